```python
import jax, jax.numpy as jnp
from jax import lax
import numpy as np

D_MODEL = 1024
BATCH = 2
SEQ = 8192
DEPTH = 1

HGRN_HEADS = 8
HGRN_DK = 128
HGRN_DV = D_MODEL // HGRN_HEADS
HGRN_K_TOTAL = HGRN_HEADS * HGRN_DK
HGRN_V_TOTAL = HGRN_HEADS * HGRN_DV
CHUNK = 64
CONV_CH = D_MODEL
CONV_K = 31
D_FF = 2816
FFN_RES = 0.5
EPS = 1e-6
IN_SPLITS = (
    HGRN_K_TOTAL,
    HGRN_K_TOTAL,
    HGRN_V_TOTAL,
    HGRN_V_TOTAL,
    CONV_CH,
    CONV_CH,
    D_MODEL,
    D_MODEL,
)
IN_COLS = sum(IN_SPLITS)

kernel_name = "hybrid_hgrn2_conformer_macaron"


def rms_norm(x, g):
    xf = x.astype(jnp.float32)
    xf = xf * lax.rsqrt(jnp.mean(xf * xf, axis=-1, keepdims=True) + EPS)
    return (xf * g.astype(jnp.float32)).astype(x.dtype)


def layer_norm(x, g, b):
    xf = x.astype(jnp.float32)
    mu = jnp.mean(xf, axis=-1, keepdims=True)
    var = jnp.mean(jnp.square(xf - mu), axis=-1, keepdims=True)
    y = (xf - mu) * lax.rsqrt(var + EPS) * g.astype(jnp.float32) + b.astype(jnp.float32)
    return y.astype(x.dtype)


def swiglu_ffn(h, w_gate, w_up, w_down):
    return (jax.nn.silu(h @ w_gate) * (h @ w_up)) @ w_down


def hgrn2_chunkwise(q, logf, k, v):
    B, L, H, dk = q.shape
    dv = v.shape[-1]
    nc = L // CHUNK

    def to_chunks(t):
        return t.reshape(B, nc, CHUNK, H, t.shape[-1]).transpose(1, 0, 3, 2, 4)

    causal = jnp.tril(jnp.ones((CHUNK, CHUNK), dtype=bool))[None, None, :, :, None]

    def step(S, inp):
        qc, lc, kc, vc = inp
        b = jnp.cumsum(lc, axis=2)
        diff = b[:, :, :, None, :] - b[:, :, None, :, :]
        decay = jnp.exp(jnp.where(causal, diff, -jnp.inf))
        scores = jnp.einsum('bhtk,bhtsk,bhsk->bhts', qc, decay, kc)
        o = jnp.einsum('bhts,bhsv->bhtv', scores, vc) \
            + jnp.einsum('bhtk,bhkv->bhtv', qc * jnp.exp(b), S)
        b_last = b[:, :, -1:, :]
        S = jnp.exp(b_last[:, :, 0, :])[..., None] * S \
            + jnp.einsum('bhsk,bhsv->bhkv', kc * jnp.exp(b_last - b), vc)
        return S, o

    S0 = jnp.zeros((B, H, dk, dv), jnp.float32)
    _, o = lax.scan(step, S0, (to_chunks(q), to_chunks(logf), to_chunks(k), to_chunks(v)))
    return o.transpose(1, 0, 3, 2, 4).reshape(B, L, H, dv)


def causal_depthwise_conv(u, w, b):
    y = lax.conv_general_dilated(
        u, w.astype(u.dtype)[:, None, :], window_strides=(1,), padding=[(CONV_K - 1, 0)],
        dimension_numbers=('NWC', 'WIO', 'NWC'), feature_group_count=u.shape[-1])
    return y + b.astype(u.dtype)


def setup_inputs(seed: int = 0) -> dict:
    key = jax.random.key(seed)
    ks = jax.random.split(key, 24)
    D, F, L = D_MODEL, D_FF, DEPTH

    def w(k, shape, fan_in):
        return jax.random.normal(k, shape, jnp.float32) * (fan_in ** -0.5)

    def gain(k, shape):
        return 1.0 + 0.05 * jax.random.normal(k, shape, jnp.float32)

    def bias(k, shape):
        return 0.02 * jax.random.normal(k, shape, jnp.float32)

    return {
        "x": jax.random.normal(ks[0], (BATCH, SEQ, D), jnp.float32),
        "ffn1_norm": gain(ks[1], (L, D)),
        "ffn1_w_gate": w(ks[2], (L, D, F), D),
        "ffn1_w_up": w(ks[3], (L, D, F), D),
        "ffn1_w_down": w(ks[4], (L, F, D), F),
        "mix_norm": gain(ks[5], (L, D)),
        "w_in": w(ks[6], (L, D, IN_COLS), D),
        "hgrn_lb_logits": 1.0 + 0.1 * jax.random.normal(ks[7], (L + 1, HGRN_K_TOTAL), jnp.float32),
        "hgrn_head_norm": gain(ks[8], (L, HGRN_DV)),
        "hgrn_w_o": w(ks[9], (L, HGRN_V_TOTAL, D), HGRN_V_TOTAL),
        "conv_w": w(ks[10], (L, CONV_K, CONV_CH), CONV_K),
        "conv_b": bias(ks[11], (L, CONV_CH)),
        "conv_ln_g": gain(ks[12], (L, CONV_CH)),
        "conv_ln_b": bias(ks[13], (L, CONV_CH)),
        "conv_w_pw": w(ks[14], (L, CONV_CH, D), CONV_CH),
        "conv_b_pw": bias(ks[15], (L, D)),
        "w_out": w(ks[16], (L, D, D), D),
        "ffn2_norm": gain(ks[17], (L, D)),
        "ffn2_w_gate": w(ks[18], (L, D, F), D),
        "ffn2_w_up": w(ks[19], (L, D, F), D),
        "ffn2_w_down": w(ks[20], (L, F, D), F),
        "final_norm": gain(ks[21], (D,)),
    }


def reference(x, ffn1_norm, ffn1_w_gate, ffn1_w_up, ffn1_w_down, mix_norm, w_in,
              hgrn_lb_logits, hgrn_head_norm, hgrn_w_o, conv_w, conv_b, conv_ln_g, conv_ln_b,
              conv_w_pw, conv_b_pw, w_out, ffn2_norm, ffn2_w_gate, ffn2_w_up, ffn2_w_down,
              final_norm):
    B, L, D = x.shape
    lb_all = jnp.cumsum(jax.nn.softmax(hgrn_lb_logits.astype(jnp.float32), axis=0), axis=0)
    split_idx = list(np.cumsum(IN_SPLITS)[:-1])

    for l in range(DEPTH):
        h = rms_norm(x, ffn1_norm[l])
        x = x + FFN_RES * swiglu_ffn(h, ffn1_w_gate[l], ffn1_w_up[l], ffn1_w_down[l])

        h = rms_norm(x, mix_norm[l])
        proj = h @ w_in[l]
        q, f_logit, i_val, g_out, c_val, c_gate, ga_logit, gb_logit = jnp.split(proj, split_idx, axis=-1)

        lb = lb_all[l]
        f = lb + (1.0 - lb) * jax.nn.sigmoid(f_logit.astype(jnp.float32))
        f = jnp.clip(f, 1e-6, 1.0)
        logf = jnp.log(f).reshape(B, L, HGRN_HEADS, HGRN_DK)
        k = (1.0 - f).reshape(B, L, HGRN_HEADS, HGRN_DK)
        qh = q.astype(jnp.float32).reshape(B, L, HGRN_HEADS, HGRN_DK)
        vh = i_val.astype(jnp.float32).reshape(B, L, HGRN_HEADS, HGRN_DV)
        o = hgrn2_chunkwise(qh, logf, k, vh).astype(x.dtype)
        o = rms_norm(o, hgrn_head_norm[l]).reshape(B, L, HGRN_V_TOTAL)
        y_a = (o * jax.nn.silu(g_out)) @ hgrn_w_o[l]

        u = c_val * jax.nn.sigmoid(c_gate)
        u = causal_depthwise_conv(u, conv_w[l], conv_b[l])
        u = jax.nn.silu(layer_norm(u, conv_ln_g[l], conv_ln_b[l]))
        y_b = u @ conv_w_pw[l] + conv_b_pw[l]

        merged = jax.nn.sigmoid(ga_logit) * y_a + jax.nn.sigmoid(gb_logit) * y_b
        x = x + merged @ w_out[l]

        h = rms_norm(x, ffn2_norm[l])
        x = x + FFN_RES * swiglu_ffn(h, ffn2_w_gate[l], ffn2_w_up[l], ffn2_w_down[l])

    return rms_norm(x, final_norm)
```

```python
import functools

import jax
import jax.numpy as jnp
from jax import lax
from jax.experimental import pallas as pl
from jax.experimental.pallas import tpu as pltpu

D_MODEL = 1024
D_FF = 2816
HEADS = 8
HEAD_DIM = 128
CONV_K = 31
FFN_RES = 0.5
EPS = 1e-6
N_SPLITS = 8

CHUNK = 64
SUB = 8
NBLK = CHUNK // SUB
TAIL = 32
MID = 3

FFN_TM = 512
MIX_TM = 256
VMEM_LIMIT = 56 * 1024 * 1024

F32 = jnp.float32
BF16 = jnp.bfloat16


def _rms(x, g):
    ms = jnp.mean(x * x, axis=-1, keepdims=True)
    return x * lax.rsqrt(ms + EPS) * g


def _resident(shape):
    nd = len(shape)
    return pl.BlockSpec(shape, lambda *_: (0,) * nd, pipeline_mode=pl.Buffered(1))


def _ffn_body(x_ref, nrm_ref, wg_ref, wu_ref, wd_ref, fin_ref, o_ref, *, final):
    x = x_ref[...]
    h = _rms(x, nrm_ref[...]).astype(BF16)
    g = jnp.dot(h, wg_ref[...], preferred_element_type=F32)
    u = jnp.dot(h, wu_ref[...], preferred_element_type=F32)
    a = (g * jax.nn.sigmoid(g) * u).astype(BF16)
    y = x + FFN_RES * jnp.dot(a, wd_ref[...], preferred_element_type=F32)
    if final:
        y = _rms(y, fin_ref[...])
    o_ref[...] = y


def _ffn(x2d, nrm, wg, wu, wd, fin, *, final):
    t, d = x2d.shape
    f = wg.shape[1]
    tm = FFN_TM
    return pl.pallas_call(
        functools.partial(_ffn_body, final=final),
        grid=(t // tm,),
        in_specs=[
            pl.BlockSpec((tm, d), lambda i: (i, 0)),
            _resident((1, d)),
            _resident((d, f)),
            _resident((d, f)),
            _resident((f, d)),
            _resident((1, d)),
        ],
        out_specs=pl.BlockSpec((tm, d), lambda i: (i, 0)),
        out_shape=jax.ShapeDtypeStruct((t, d), F32),
        compiler_params=pltpu.CompilerParams(
            dimension_semantics=("arbitrary",), vmem_limit_bytes=VMEM_LIMIT),
        name="ffn_final" if final else "ffn",
    )(x2d, nrm, wg, wu, wd, fin)


def _hgrn_chunk(c, carry, *, q_s, k_s, lf_s, v_s, o_s, st_ref):
    r0 = pl.multiple_of(c * CHUNK, CHUNK)
    rows = pl.ds(r0, CHUNK)
    lf = lf_s[rows, :]
    q = q_s[rows, :]
    k = k_s[rows, :]
    v = v_s[rows, :].astype(BF16)

    rid = lax.broadcasted_iota(jnp.int32, (CHUNK, D_MODEL), 0) % SUB
    a = lf
    for s in (1, 2, 4):
        a = a + jnp.where(rid >= s, pltpu.roll(a, s, 0), 0.0)

    qe, kf, qd, kd, g = [], [], [], [], []
    for i in range(NBLK):
        sl = slice(i * SUB, (i + 1) * SUB)
        a_i, q_i, k_i = a[sl], q[sl], k[sl]
        tot = a_i[SUB - 1:SUB]
        mid = a_i[MID:MID + 1]
        e_i = jnp.exp(a_i)
        qe.append(q_i * e_i)
        kf.append(k_i * jnp.exp(tot - a_i))
        qd.append(q_i * jnp.exp(a_i - mid))
        kd.append(k_i * jnp.exp(mid - a_i))
        g.append(e_i[SUB - 1:SUB])

    def prod(items):
        out = None
        for it in items:
            out = it if out is None else out * it
        return out

    def scaled(base, i, factors):
        p = prod(factors)
        return base[i] if p is None else base[i] * p

    def level_q(w):
        nb = w // SUB
        return [scaled(qe, i, g[(i // nb) * nb:i]) for i in range(NBLK)]

    def level_k(w):
        nb = w // SUB
        return [scaled(kf, i, g[i + 1:(i // nb + 1) * nb]) for i in range(NBLK)]

    def cat(blocks):
        return jnp.concatenate(blocks, axis=0).astype(BF16)

    q_d, k_d = cat(qd), cat(kd)
    q_8, k_8 = cat(qe), cat(kf)
    q_16, k_16 = cat(level_q(16)), cat(level_k(16))
    q_32, k_32 = cat(level_q(32)), cat(level_k(32))
    q_c, k_c = cat(level_q(CHUNK)), cat(level_k(CHUNK))
    g_tot = prod(g)

    ti = lax.broadcasted_iota(jnp.int32, (CHUNK, CHUNK), 0)
    si = lax.broadcasted_iota(jnp.int32, (CHUNK, CHUNK), 1)
    bt, bs = ti // SUB, si // SUB
    m_d = (bt == bs) & (si <= ti)
    m_8 = (bt == bs + 1) & (bt % 2 == 1)
    m_16 = (bt // 4 == bs // 4) & ((bt // 2) % 2 == 1) & ((bs // 2) % 2 == 0)
    m_32 = (bt // 4 == 1) & (bs // 4 == 0)

    nt = (((1,), (1,)), ((), ()))
    tn = (((0,), (0,)), ((), ()))
    for h in range(HEADS):
        hs = slice(h * HEAD_DIM, (h + 1) * HEAD_DIM)

        def score(qq, kk):
            return lax.dot_general(qq[:, hs], kk[:, hs], nt, preferred_element_type=F32)

        p = jnp.where(m_d, score(q_d, k_d),
                      jnp.where(m_8, score(q_8, k_8),
                                jnp.where(m_16, score(q_16, k_16),
                                          jnp.where(m_32, score(q_32, k_32), 0.0))))
        st = st_ref[h]
        o_h = jnp.dot(p.astype(BF16), v[:, hs], preferred_element_type=F32)
        o_h = o_h + lax.dot_general(q_c[:, hs], st.astype(BF16), nt, preferred_element_type=F32)
        o_s[rows, hs] = o_h
        st_ref[h] = st * g_tot[:, hs] + lax.dot_general(
            v[:, hs], k_c[:, hs], tn, preferred_element_type=F32)
    return carry


def _conv_block(r, carry, *, uext_ref, cw_ref, cb_ref, y_s, rows_per):
    r0 = pl.multiple_of(r * rows_per, rows_per)
    acc = jnp.zeros((rows_per, D_MODEL), F32) + cb_ref[...]
    win = uext_ref[pl.ds(r0, rows_per + TAIL), :]
    for j in range(CONV_K):
        lo = TAIL - CONV_K + 1 + j
        acc = acc + win[lo:lo + rows_per] * cw_ref[j:j + 1, :]
    y_s[pl.ds(r0, rows_per), :] = acc
    return carry


def _mixer_body(x_ref, mixn_ref, win_ref, lbl_ref, hn_ref, wo_ref, cw_ref, cb_ref, lng_ref,
                lnb_ref, wpw_ref, bpw_ref, wout_ref, o_ref,
                st_ref, uext_ref, q_s, k_s, lf_s, v_s, o_s, y_s, *, layer):
    tm = x_ref.shape[1]

    @pl.when(pl.program_id(1) == 0)
    def _():
        st_ref[...] = jnp.zeros_like(st_ref)
        uext_ref[0:TAIL, :] = jnp.zeros((TAIL, D_MODEL), F32)

    x = x_ref[0]
    h = _rms(x, mixn_ref[...]).astype(BF16)

    def proj(i):
        return jnp.dot(h, win_ref[:, i * D_MODEL:(i + 1) * D_MODEL], preferred_element_type=F32)

    lbl = lbl_ref[...]
    e = jnp.exp(lbl - jnp.max(lbl, axis=0, keepdims=True))
    lb = jnp.sum(e[0:layer + 1], axis=0, keepdims=True) / jnp.sum(e, axis=0, keepdims=True)

    f = lb + (1.0 - lb) * jax.nn.sigmoid(proj(1))
    f = jnp.clip(f, 1e-6, 1.0)
    lf_s[...] = jnp.log(f)
    k_s[...] = 1.0 - f
    q_s[...] = proj(0)
    v_s[...] = proj(2)

    lax.fori_loop(0, tm // CHUNK,
                  functools.partial(_hgrn_chunk, q_s=q_s, k_s=k_s, lf_s=lf_s, v_s=v_s, o_s=o_s,
                                    st_ref=st_ref), 0)

    o = o_s[...]
    hn = hn_ref[...]
    o = jnp.concatenate(
        [_rms(o[:, i * HEAD_DIM:(i + 1) * HEAD_DIM], hn) for i in range(HEADS)], axis=1)
    g_out = proj(3)
    y_a = jnp.dot((o * (g_out * jax.nn.sigmoid(g_out))).astype(BF16), wo_ref[...],
                  preferred_element_type=F32)

    uext_ref[TAIL:TAIL + tm, :] = proj(4) * jax.nn.sigmoid(proj(5))
    rows_per = 16
    lax.fori_loop(0, tm // rows_per,
                  functools.partial(_conv_block, uext_ref=uext_ref, cw_ref=cw_ref, cb_ref=cb_ref,
                                    y_s=y_s, rows_per=rows_per), 0)
    uext_ref[0:TAIL, :] = uext_ref[tm:tm + TAIL, :]
    u = y_s[...]
    mu = jnp.mean(u, axis=-1, keepdims=True)
    uc = u - mu
    var = jnp.mean(uc * uc, axis=-1, keepdims=True)
    u = uc * lax.rsqrt(var + EPS) * lng_ref[...] + lnb_ref[...]
    u = u * jax.nn.sigmoid(u)
    y_b = jnp.dot(u.astype(BF16), wpw_ref[...], preferred_element_type=F32) + bpw_ref[...]

    merged = jax.nn.sigmoid(proj(6)) * y_a + jax.nn.sigmoid(proj(7)) * y_b
    o_ref[0] = x + jnp.dot(merged.astype(BF16), wout_ref[...], preferred_element_type=F32)


def _mixer(x, mixn, win, lbl, hn, wo, cw, cb, lng, lnb, wpw, bpw, wout, *, layer):
    b, l, d = x.shape
    tm = MIX_TM
    act = lambda: pltpu.VMEM((tm, d), F32)
    return pl.pallas_call(
        functools.partial(_mixer_body, layer=layer),
        grid=(b, l // tm),
        in_specs=[
            pl.BlockSpec((1, tm, d), lambda i, j: (i, j, 0)),
            _resident(mixn.shape), _resident(win.shape), _resident(lbl.shape),
            _resident(hn.shape), _resident(wo.shape), _resident(cw.shape), _resident(cb.shape),
            _resident(lng.shape), _resident(lnb.shape), _resident(wpw.shape),
            _resident(bpw.shape), _resident(wout.shape),
        ],
        out_specs=pl.BlockSpec((1, tm, d), lambda i, j: (i, j, 0)),
        out_shape=jax.ShapeDtypeStruct((b, l, d), F32),
        scratch_shapes=[
            pltpu.VMEM((HEADS, HEAD_DIM, HEAD_DIM), F32),
            pltpu.VMEM((TAIL + tm, d), F32),
            act(), act(), act(), act(), act(), act(),
        ],
        compiler_params=pltpu.CompilerParams(
            dimension_semantics=("arbitrary", "arbitrary"), vmem_limit_bytes=VMEM_LIMIT),
        name="mixer",
    )(x, mixn, win, lbl, hn, wo, cw, cb, lng, lnb, wpw, bpw, wout)


def kernel(x, ffn1_norm, ffn1_w_gate, ffn1_w_up, ffn1_w_down, mix_norm, w_in, hgrn_lb_logits, hgrn_head_norm, hgrn_w_o, conv_w, conv_b, conv_ln_g, conv_ln_b, conv_w_pw, conv_b_pw, w_out, ffn2_norm, ffn2_w_gate, ffn2_w_up, ffn2_w_down, final_norm):
    b, l, d = x.shape
    depth = ffn1_norm.shape[0]
    row = lambda a: a.reshape(1, -1)
    w16 = lambda a: a.astype(BF16)
    fin = row(final_norm)
    for i in range(depth):
        x = _ffn(x.reshape(b * l, d), row(ffn1_norm[i]), w16(ffn1_w_gate[i]), w16(ffn1_w_up[i]),
                 w16(ffn1_w_down[i]), fin, final=False).reshape(b, l, d)
        x = _mixer(x, row(mix_norm[i]), w16(w_in[i]), hgrn_lb_logits, row(hgrn_head_norm[i]),
                   w16(hgrn_w_o[i]), conv_w[i], row(conv_b[i]), row(conv_ln_g[i]),
                   row(conv_ln_b[i]), w16(conv_w_pw[i]), row(conv_b_pw[i]), w16(w_out[i]), layer=i)
        last = i == depth - 1
        x = _ffn(x.reshape(b * l, d), row(ffn2_norm[i]), w16(ffn2_w_gate[i]), w16(ffn2_w_up[i]),
                 w16(ffn2_w_down[i]), fin, final=last).reshape(b, l, d)
    return x
```

```python
import functools

import jax
import jax.numpy as jnp
from jax import lax
from jax.experimental import pallas as pl
from jax.experimental.pallas import tpu as pltpu

D_MODEL = 1024
D_FF = 2816
HEADS = 8
HEAD_DIM = 128
CONV_K = 31
FFN_RES = 0.5
EPS = 1e-6
N_SPLITS = 8

CHUNK = 64
SUB = 8
NBLK = CHUNK // SUB
TAIL = 32
MID = 3

FFN_TM = 512
MIX_TM = 256
VMEM_LIMIT = 56 * 1024 * 1024

F32 = jnp.float32
BF16 = jnp.bfloat16


def _rms(x, g):
    ms = jnp.mean(x * x, axis=-1, keepdims=True)
    return x * lax.rsqrt(ms + EPS) * g


def _resident(shape):
    nd = len(shape)
    return pl.BlockSpec(shape, lambda *_: (0,) * nd, pipeline_mode=pl.Buffered(1))


def _ffn_body(x_ref, nrm_ref, wg_ref, wu_ref, wd_ref, fin_ref, o_ref, *, final):
    x = x_ref[...]
    h = _rms(x, nrm_ref[...]).astype(BF16)
    g = jnp.dot(h, wg_ref[...], preferred_element_type=F32)
    u = jnp.dot(h, wu_ref[...], preferred_element_type=F32)
    a = (g * jax.nn.sigmoid(g) * u).astype(BF16)
    y = x + FFN_RES * jnp.dot(a, wd_ref[...], preferred_element_type=F32)
    if final:
        y = _rms(y, fin_ref[...])
    o_ref[...] = y


def _ffn(x2d, nrm, wg, wu, wd, fin, *, final):
    t, d = x2d.shape
    f = wg.shape[1]
    tm = FFN_TM
    return pl.pallas_call(
        functools.partial(_ffn_body, final=final),
        grid=(t // tm,),
        in_specs=[
            pl.BlockSpec((tm, d), lambda i: (i, 0)),
            _resident((1, d)),
            _resident((d, f)),
            _resident((d, f)),
            _resident((f, d)),
            _resident((1, d)),
        ],
        out_specs=pl.BlockSpec((tm, d), lambda i: (i, 0)),
        out_shape=jax.ShapeDtypeStruct((t, d), F32),
        compiler_params=pltpu.CompilerParams(
            dimension_semantics=("arbitrary",), vmem_limit_bytes=VMEM_LIMIT),
        name="ffn_final" if final else "ffn",
    )(x2d, nrm, wg, wu, wd, fin)


def _hgrn_chunk(c, carry, *, q_s, k_s, lf_s, v_s, o_s, st_ref):
    r0 = pl.multiple_of(c * CHUNK, CHUNK)
    rows = pl.ds(r0, CHUNK)
    lf = lf_s[rows, :]
    q = q_s[rows, :]
    k = k_s[rows, :]
    v = v_s[rows, :].astype(BF16)

    rid = lax.broadcasted_iota(jnp.int32, (CHUNK, D_MODEL), 0) % SUB
    a = lf
    for s in (1, 2, 4):
        a = a + jnp.where(rid >= s, pltpu.roll(a, s, 0), 0.0)

    qe, kf, qd, kd, g = [], [], [], [], []
    for i in range(NBLK):
        sl = slice(i * SUB, (i + 1) * SUB)
        a_i, q_i, k_i = a[sl], q[sl], k[sl]
        tot = a_i[SUB - 1:SUB]
        mid = a_i[MID:MID + 1]
        e_i = jnp.exp(a_i)
        qe.append(q_i * e_i)
        kf.append(k_i * jnp.exp(tot - a_i))
        qd.append(q_i * jnp.exp(a_i - mid))
        kd.append(k_i * jnp.exp(mid - a_i))
        g.append(e_i[SUB - 1:SUB])

    def prod(items):
        out = None
        for it in items:
            out = it if out is None else out * it
        return out

    def scaled(base, i, factors):
        p = prod(factors)
        return base[i] if p is None else base[i] * p

    def level_q(w):
        nb = w // SUB
        return [scaled(qe, i, g[(i // nb) * nb:i]) for i in range(NBLK)]

    def level_k(w):
        nb = w // SUB
        return [scaled(kf, i, g[i + 1:(i // nb + 1) * nb]) for i in range(NBLK)]

    def cat(blocks):
        return jnp.concatenate(blocks, axis=0).astype(BF16)

    q_d, k_d = cat(qd), cat(kd)
    q_8, k_8 = cat(qe), cat(kf)
    q_16, k_16 = cat(level_q(16)), cat(level_k(16))
    q_32, k_32 = cat(level_q(32)), cat(level_k(32))
    q_c, k_c = cat(level_q(CHUNK)), cat(level_k(CHUNK))
    g_tot = prod(g)

    ti = lax.broadcasted_iota(jnp.int32, (CHUNK, CHUNK), 0)
    si = lax.broadcasted_iota(jnp.int32, (CHUNK, CHUNK), 1)
    bt, bs = ti // SUB, si // SUB
    m_d = (bt == bs) & (si <= ti)
    m_8 = (bt == bs + 1) & (bt % 2 == 1)
    m_16 = (bt // 4 == bs // 4) & ((bt // 2) % 2 == 1) & ((bs // 2) % 2 == 0)
    m_32 = (bt // 4 == 1) & (bs // 4 == 0)

    nt = (((1,), (1,)), ((), ()))
    tn = (((0,), (0,)), ((), ()))
    for h in range(HEADS):
        hs = slice(h * HEAD_DIM, (h + 1) * HEAD_DIM)

        def score(qq, kk):
            return lax.dot_general(qq[:, hs], kk[:, hs], nt, preferred_element_type=F32)

        p = jnp.where(m_d, score(q_d, k_d),
                      jnp.where(m_8, score(q_8, k_8),
                                jnp.where(m_16, score(q_16, k_16),
                                          jnp.where(m_32, score(q_32, k_32), 0.0))))
        st = st_ref[h]
        o_h = jnp.dot(p.astype(BF16), v[:, hs], preferred_element_type=F32)
        o_h = o_h + lax.dot_general(q_c[:, hs], st.astype(BF16), nt, preferred_element_type=F32)
        o_s[rows, hs] = o_h
        st_ref[h] = st * g_tot[:, hs] + lax.dot_general(
            v[:, hs], k_c[:, hs], tn, preferred_element_type=F32)
    return carry


def _conv_block(r, carry, *, ush_ref, cw_ref, cb_ref, y_s, rows_per):
    r0 = pl.multiple_of(r * rows_per, rows_per)
    nsub = rows_per // SUB
    acc = [cb_ref[...]] * nsub
    for j in range(CONV_K):
        off = TAIL - CONV_K + 1 + j
        w_j = cw_ref[j]
        for b in range(nsub):
            start = pl.multiple_of(r0 + (b + 1 + off // SUB) * SUB, SUB)
            acc[b] = acc[b] + ush_ref[off % SUB, pl.ds(start, SUB), :] * w_j
    for b in range(nsub):
        y_s[pl.ds(pl.multiple_of(r0 + b * SUB, SUB), SUB), :] = acc[b]
    return carry


def _mixer_body(x_ref, mixn_ref, win_ref, lbl_ref, hn_ref, wo_ref, cw_ref, cb_ref, lng_ref,
                lnb_ref, wpw_ref, bpw_ref, wout_ref, o_ref,
                st_ref, tail_ref, ush_ref, q_s, k_s, lf_s, v_s, o_s, y_s, *, layer):
    tm = x_ref.shape[1]

    @pl.when(pl.program_id(1) == 0)
    def _():
        st_ref[...] = jnp.zeros_like(st_ref)
        tail_ref[...] = jnp.zeros_like(tail_ref)

    x = x_ref[0]
    h = _rms(x, mixn_ref[...]).astype(BF16)

    def proj(i):
        return jnp.dot(h, win_ref[:, i * D_MODEL:(i + 1) * D_MODEL], preferred_element_type=F32)

    lbl = lbl_ref[...]
    e = jnp.exp(lbl - jnp.max(lbl, axis=0, keepdims=True))
    lb = jnp.sum(e[0:layer + 1], axis=0, keepdims=True) / jnp.sum(e, axis=0, keepdims=True)

    f = lb + (1.0 - lb) * jax.nn.sigmoid(proj(1))
    f = jnp.clip(f, 1e-6, 1.0)
    lf_s[...] = jnp.log(f)
    k_s[...] = 1.0 - f
    q_s[...] = proj(0)
    v_s[...] = proj(2)

    lax.fori_loop(0, tm // CHUNK,
                  functools.partial(_hgrn_chunk, q_s=q_s, k_s=k_s, lf_s=lf_s, v_s=v_s, o_s=o_s,
                                    st_ref=st_ref), 0)

    o = o_s[...]
    hn = hn_ref[...]
    o = jnp.concatenate(
        [_rms(o[:, i * HEAD_DIM:(i + 1) * HEAD_DIM], hn) for i in range(HEADS)], axis=1)
    g_out = proj(3)
    y_a = jnp.dot((o * (g_out * jax.nn.sigmoid(g_out))).astype(BF16), wo_ref[...],
                  preferred_element_type=F32)

    u = proj(4) * jax.nn.sigmoid(proj(5))
    ext = jnp.concatenate([tail_ref[...], u], axis=0)
    for r in range(SUB):
        ush_ref[r, SUB - r:SUB - r + TAIL + tm, :] = ext
    tail_ref[...] = u[tm - TAIL:tm]
    rows_per = 32
    lax.fori_loop(0, tm // rows_per,
                  functools.partial(_conv_block, ush_ref=ush_ref, cw_ref=cw_ref, cb_ref=cb_ref,
                                    y_s=y_s, rows_per=rows_per), 0)
    u = y_s[...]
    mu = jnp.mean(u, axis=-1, keepdims=True)
    uc = u - mu
    var = jnp.mean(uc * uc, axis=-1, keepdims=True)
    u = uc * lax.rsqrt(var + EPS) * lng_ref[...] + lnb_ref[...]
    u = u * jax.nn.sigmoid(u)
    y_b = jnp.dot(u.astype(BF16), wpw_ref[...], preferred_element_type=F32) + bpw_ref[...]

    merged = jax.nn.sigmoid(proj(6)) * y_a + jax.nn.sigmoid(proj(7)) * y_b
    o_ref[0] = x + jnp.dot(merged.astype(BF16), wout_ref[...], preferred_element_type=F32)


def _mixer(x, mixn, win, lbl, hn, wo, cw, cb, lng, lnb, wpw, bpw, wout, *, layer):
    b, l, d = x.shape
    tm = MIX_TM
    act = lambda: pltpu.VMEM((tm, d), F32)
    return pl.pallas_call(
        functools.partial(_mixer_body, layer=layer),
        grid=(b, l // tm),
        in_specs=[
            pl.BlockSpec((1, tm, d), lambda i, j: (i, j, 0)),
            _resident(mixn.shape), _resident(win.shape), _resident(lbl.shape),
            _resident(hn.shape), _resident(wo.shape), _resident(cw.shape), _resident(cb.shape),
            _resident(lng.shape), _resident(lnb.shape), _resident(wpw.shape),
            _resident(bpw.shape), _resident(wout.shape),
        ],
        out_specs=pl.BlockSpec((1, tm, d), lambda i, j: (i, j, 0)),
        out_shape=jax.ShapeDtypeStruct((b, l, d), F32),
        scratch_shapes=[
            pltpu.VMEM((HEADS, HEAD_DIM, HEAD_DIM), F32),
            pltpu.VMEM((TAIL, d), F32),
            pltpu.VMEM((SUB, SUB + TAIL + tm, d), F32),
            act(), act(), act(), act(), act(), act(),
        ],
        compiler_params=pltpu.CompilerParams(
            dimension_semantics=("arbitrary", "arbitrary"), vmem_limit_bytes=VMEM_LIMIT),
        name="mixer",
    )(x, mixn, win, lbl, hn, wo, cw, cb, lng, lnb, wpw, bpw, wout)


def kernel(x, ffn1_norm, ffn1_w_gate, ffn1_w_up, ffn1_w_down, mix_norm, w_in, hgrn_lb_logits, hgrn_head_norm, hgrn_w_o, conv_w, conv_b, conv_ln_g, conv_ln_b, conv_w_pw, conv_b_pw, w_out, ffn2_norm, ffn2_w_gate, ffn2_w_up, ffn2_w_down, final_norm):
    b, l, d = x.shape
    depth = ffn1_norm.shape[0]
    row = lambda a: a.reshape(1, -1)
    w16 = lambda a: a.astype(BF16)
    sub8 = lambda a: jnp.broadcast_to(a[..., None, :], a.shape[:-1] + (SUB, a.shape[-1]))
    fin = row(final_norm)
    for i in range(depth):
        x = _ffn(x.reshape(b * l, d), row(ffn1_norm[i]), w16(ffn1_w_gate[i]), w16(ffn1_w_up[i]),
                 w16(ffn1_w_down[i]), fin, final=False).reshape(b, l, d)
        x = _mixer(x, row(mix_norm[i]), w16(w_in[i]), hgrn_lb_logits, row(hgrn_head_norm[i]),
                   w16(hgrn_w_o[i]), sub8(conv_w[i]), sub8(conv_b[i]), row(conv_ln_g[i]),
                   row(conv_ln_b[i]), w16(conv_w_pw[i]), row(conv_b_pw[i]), w16(w_out[i]), layer=i)
        last = i == depth - 1
        x = _ffn(x.reshape(b * l, d), row(ffn2_norm[i]), w16(ffn2_w_gate[i]), w16(ffn2_w_up[i]),
                 w16(ffn2_w_down[i]), fin, final=last).reshape(b, l, d)
    return x
```

```python
import functools

import jax
import jax.numpy as jnp
from jax import lax
from jax.experimental import pallas as pl
from jax.experimental.pallas import tpu as pltpu

D_MODEL = 1024
D_FF = 2816
HEADS = 8
HEAD_DIM = 128
PAIR = 2 * HEAD_DIM
CONV_K = 31
FFN_RES = 0.5
EPS = 1e-6
N_SPLITS = 8

CHUNK = 64
SUB = 8
NBLK = CHUNK // SUB
TAIL = 32
MID = 3
CONV_ROWS = 32
CONV_LANES = 256
LEVELS = ("d", "8", "16", "32")

FFN_TM = 512
MIX_TM = 256
VMEM_LIMIT = 60 * 1024 * 1024

F32 = jnp.float32
BF16 = jnp.bfloat16
NT = (((1,), (1,)), ((), ()))
TN = (((0,), (0,)), ((), ()))


def _rms(x, g):
    ms = jnp.mean(x * x, axis=-1, keepdims=True)
    return x * lax.rsqrt(ms + EPS) * g


def _sigmoid(x):
    return 0.5 * jnp.tanh(0.5 * x) + 0.5


def _block_rows(i, n):
    if isinstance(i, int):
        return pl.ds(i * n, n)
    return pl.ds(pl.multiple_of(i * n, n), n)


def _resident(shape):
    nd = len(shape)
    return pl.BlockSpec(shape, lambda *_: (0,) * nd, pipeline_mode=pl.Buffered(1))


def _ffn_body(x_ref, nrm_ref, wg_ref, wu_ref, wd_ref, fin_ref, o_ref, *, final):
    x = x_ref[...]
    h = _rms(x, nrm_ref[...]).astype(BF16)
    g = jnp.dot(h, wg_ref[...], preferred_element_type=F32)
    u = jnp.dot(h, wu_ref[...], preferred_element_type=F32)
    a = (g * _sigmoid(g) * u).astype(BF16)
    y = x + FFN_RES * jnp.dot(a, wd_ref[...], preferred_element_type=F32)
    if final:
        y = _rms(y, fin_ref[...])
    o_ref[...] = y


def _ffn(x2d, nrm, wg, wu, wd, fin, *, final):
    t, d = x2d.shape
    f = wg.shape[1]
    tm = FFN_TM
    return pl.pallas_call(
        functools.partial(_ffn_body, final=final),
        grid=(t // tm,),
        in_specs=[
            pl.BlockSpec((tm, d), lambda i: (i, 0)),
            _resident((1, d)),
            _resident((d, f)),
            _resident((d, f)),
            _resident((f, d)),
            _resident((1, d)),
        ],
        out_specs=pl.BlockSpec((tm, d), lambda i: (i, 0)),
        out_shape=jax.ShapeDtypeStruct((t, d), F32),
        compiler_params=pltpu.CompilerParams(
            dimension_semantics=("arbitrary",), vmem_limit_bytes=VMEM_LIMIT),
        name="ffn_final" if final else "ffn",
    )(x2d, nrm, wg, wu, wd, fin)


def _hgrn_prep(c, carry, *, q_s, k_s, lf_s, qv, kv, gt_s):
    rows = _block_rows(c, CHUNK)
    lf, q, k = lf_s[rows, :], q_s[rows, :], k_s[rows, :]
    rid = lax.broadcasted_iota(jnp.int32, (SUB, D_MODEL), 0)

    qe, kf, qd, kd, g = [], [], [], [], []
    for i in range(NBLK):
        sl = slice(i * SUB, (i + 1) * SUB)
        a = lf[sl]
        for s in (1, 2, 4):
            a = a + jnp.where(rid >= s, pltpu.roll(a, s, 0), 0.0)
        tot = jnp.broadcast_to(a[SUB - 1:SUB], (SUB, D_MODEL))
        dm = a - jnp.broadcast_to(a[MID:MID + 1], (SUB, D_MODEL))
        qe.append(q[sl] * jnp.exp(a))
        kf.append(k[sl] * jnp.exp(tot - a))
        qd.append(q[sl] * jnp.exp(dm))
        kd.append(k[sl] * jnp.exp(-dm))
        g.append(jnp.exp(tot))

    def running(blocks):
        out, acc = [None], None
        for blk in blocks:
            acc = blk if acc is None else acc * blk
            out.append(acc)
        return out

    def scaled(base, factors):
        return [b if f is None else b * f for b, f in zip(base, factors)]

    def store(ref, blocks):
        ref[rows, :] = jnp.concatenate(blocks, axis=0).astype(BF16)

    def q_factors(nb):
        out = []
        for lo in range(0, NBLK, nb):
            out += running(g[lo:lo + nb])[:nb]
        return out

    def k_factors(nb):
        out = []
        for lo in range(0, NBLK, nb):
            out += running(g[lo:lo + nb][::-1])[:nb][::-1]
        return out

    store(qv["d"], qd)
    store(kv["d"], kd)
    store(qv["8"], qe)
    store(kv["8"], kf)
    for name, nb in (("16", 2), ("32", 4), ("c", NBLK)):
        store(qv[name], scaled(qe, q_factors(nb)))
        store(kv[name], scaled(kf, k_factors(nb)))
    gt_s[_block_rows(c, SUB), :] = running(g)[NBLK]
    return carry


def _hgrn_mm(c, carry, *, qv, kv, og_s, o_s, st_ref, gt_s):
    rows = _block_rows(c, CHUNK)
    gt = gt_s[_block_rows(c, SUB), :]
    gw = D_MODEL // og_s.shape[0]
    v = [jnp.concatenate([og_s[r, rows, 0:gw] for r in range(p * PAIR // gw, (p + 1) * PAIR // gw)],
                         axis=1).astype(BF16) for p in range(HEADS // 2)]

    ti = lax.broadcasted_iota(jnp.int32, (CHUNK, 2 * CHUNK), 0)
    si = lax.broadcasted_iota(jnp.int32, (CHUNK, 2 * CHUNK), 1) % CHUNK
    bt, bs = ti // SUB, si // SUB
    masks = {
        "d": (bt == bs) & (si <= ti),
        "8": (bt == bs + 1) & (bt % 2 == 1),
        "16": (bt // 4 == bs // 4) & ((bt // 2) % 2 == 1) & ((bs // 2) % 2 == 0),
        "32": (bt // 4 == 1) & (bs // 4 == 0),
    }

    def bdiag(x):
        z = jnp.zeros((x.shape[0], HEAD_DIM), x.dtype)
        return jnp.concatenate(
            [jnp.concatenate([x[:, :HEAD_DIM], z], axis=1),
             jnp.concatenate([z, x[:, HEAD_DIM:]], axis=1)], axis=0)

    pairs = [slice(p * PAIR, (p + 1) * PAIR) for p in range(HEADS // 2)]
    scores = [{name: lax.dot_general(qv[name][rows, ps], bdiag(kv[name][rows, ps]), NT,
                                     preferred_element_type=F32) for name in LEVELS}
              for ps in pairs]
    states = [(st_ref[2 * p], st_ref[2 * p + 1]) for p in range(len(pairs))]
    inter = [lax.dot_general(qv["c"][rows, ps],
                             bdiag(jnp.concatenate(st, axis=1).astype(BF16)), NT,
                             preferred_element_type=F32) for ps, st in zip(pairs, states)]
    upds = [lax.dot_general(v[p], kv["c"][rows, ps], TN, preferred_element_type=F32)
            for p, ps in enumerate(pairs)]
    for p, ps in enumerate(pairs):
        pm = jnp.zeros((CHUNK, 2 * CHUNK), F32)
        for name in LEVELS:
            pm = jnp.where(masks[name], scores[p][name], pm)
        o_s[rows, ps] = inter[p] + jnp.dot(pm.astype(BF16), bdiag(v[p]),
                                           preferred_element_type=F32)
        g_p = jnp.tile(gt[:, ps], (HEAD_DIM // SUB, 1))
        st0, st1 = states[p]
        st_ref[2 * p] = st0 * g_p[:, :HEAD_DIM] + upds[p][:HEAD_DIM, :HEAD_DIM]
        st_ref[2 * p + 1] = st1 * g_p[:, HEAD_DIM:] + upds[p][HEAD_DIM:, HEAD_DIM:]
    return carry


def _conv_block(r, carry, *, ush_ref, cw_ref, cb_ref, y_s, rows_per, between=None):
    r0 = pl.multiple_of(r * rows_per, rows_per)
    nsub = rows_per // SUB
    lead = TAIL - CONV_K + 1
    for lo in range(0, D_MODEL, CONV_LANES):
        lanes = slice(lo, lo + CONV_LANES)
        acc = [cb_ref[:, lanes]] * nsub
        for shift in range(SUB):
            taps = [j for j in range(CONV_K) if (lead + j) % SUB == shift]
            w = {j: cw_ref[j, :, lanes] for j in taps}
            first = min((lead + j) // SUB for j in taps)
            last = max((lead + j) // SUB for j in taps) + nsub - 1
            for m in range(first, last + 1):
                start = pl.multiple_of(r0 + (m + 1) * SUB, SUB)
                blk = ush_ref[shift, pl.ds(start, SUB), lanes]
                for j in taps:
                    b = m - (lead + j) // SUB
                    if 0 <= b < nsub:
                        acc[b] = acc[b] + blk * w[j]
        for b in range(nsub):
            y_s[pl.ds(pl.multiple_of(r0 + b * SUB, SUB), SUB), lanes] = acc[b]
        if between is not None:
            between(lo // CONV_LANES)
    return carry


def _mixer_body(x_ref, mixn_ref, wqf_ref, wa_ref, wg_ref, lbl_ref, hn_ref, wo_ref, cw_ref, cb_ref,
                lng_ref, lnb_ref, wpw_ref, bpw_ref, wout_ref, o_ref,
                st_ref, tail_ref, ush_ref, gt_s, h_s, oa_s, og_s, q_s, k_s, lf_s, o_s, y_s,
                *var_refs, layer):
    tm = x_ref.shape[1]
    names = LEVELS + ("c",)
    qv = dict(zip(names, var_refs[:len(names)]))
    kv = dict(zip(names, var_refs[len(names):]))
    n_chunks, n_conv = oa_s.shape[0], og_s.shape[0]

    @pl.when(pl.program_id(1) == 0)
    def _():
        st_ref[...] = jnp.zeros_like(st_ref)
        tail_ref[...] = jnp.zeros_like(tail_ref)

    x = x_ref[0]
    h = _rms(x, mixn_ref[...]).astype(BF16)
    h_s[...] = h

    lbl = lbl_ref[...]
    e = jnp.exp(lbl - jnp.max(lbl, axis=0, keepdims=True))
    lb = jnp.sum(e[0:layer + 1], axis=0, keepdims=True) / jnp.sum(e, axis=0, keepdims=True)

    qf = jnp.dot(h, wqf_ref[...], preferred_element_type=F32)
    f = lb + (1.0 - lb) * _sigmoid(qf[:, D_MODEL:])
    f = jnp.clip(f, 1e-6, 1.0)
    lf_s[...] = jnp.log(f)
    k_s[...] = 1.0 - f
    q_s[...] = qf[:, :D_MODEL]

    def prep_and_project(c, carry):
        _hgrn_prep(c, carry, q_s=q_s, k_s=k_s, lf_s=lf_s, qv=qv, kv=kv, gt_s=gt_s)
        oa_s[c] = jnp.dot(h_s[...], wa_ref[c], preferred_element_type=F32)
        return carry

    lax.fori_loop(0, n_chunks, prep_and_project, 0)

    cw = D_MODEL // n_chunks
    u = jnp.concatenate(
        [oa_s[c, :, 0:cw] * _sigmoid(oa_s[c, :, cw:2 * cw]) for c in range(n_chunks)], axis=1)
    ext = jnp.concatenate([tail_ref[...], u], axis=0)
    for r in range(SUB):
        ush_ref[r, SUB - r:SUB - r + TAIL + tm, :] = ext
    tail_ref[...] = u[tm - TAIL:tm]

    def conv_and_project(r, carry):
        def project(i):
            if i % 2 == 0:
                cols = slice(i // 2 * PAIR, (i // 2 + 1) * PAIR)
                og_s[r, :, cols] = jnp.dot(h_s[...], wg_ref[r, :, cols],
                                           preferred_element_type=F32)

        return _conv_block(r, carry, ush_ref=ush_ref, cw_ref=cw_ref, cb_ref=cb_ref, y_s=y_s,
                           rows_per=tm // n_conv, between=project)

    lax.fori_loop(0, n_conv, conv_and_project, 0)
    lax.fori_loop(0, n_chunks,
                  functools.partial(_hgrn_mm, qv=qv, kv=kv, og_s=og_s, o_s=o_s, st_ref=st_ref,
                                    gt_s=gt_s), 0)
    gw = D_MODEL // n_conv
    g_out, ga, gb = (jnp.concatenate(
        [og_s[r, :, i * gw:(i + 1) * gw] for r in range(n_conv)], axis=1) for i in (1, 2, 3))

    o = o_s[...]
    hn = hn_ref[...]
    o = jnp.concatenate(
        [_rms(o[:, i * HEAD_DIM:(i + 1) * HEAD_DIM], hn) for i in range(HEADS)], axis=1)
    y_a = jnp.dot((o * (g_out * _sigmoid(g_out))).astype(BF16), wo_ref[...],
                  preferred_element_type=F32)

    u = y_s[...]
    mu = jnp.mean(u, axis=-1, keepdims=True)
    uc = u - mu
    var = jnp.mean(uc * uc, axis=-1, keepdims=True)
    u = uc * lax.rsqrt(var + EPS) * lng_ref[...] + lnb_ref[...]
    u = u * _sigmoid(u)
    y_b = jnp.dot(u.astype(BF16), wpw_ref[...], preferred_element_type=F32) + bpw_ref[...]

    merged = _sigmoid(ga) * y_a + _sigmoid(gb) * y_b
    o_ref[0] = x + jnp.dot(merged.astype(BF16), wout_ref[...], preferred_element_type=F32)


def _column_slices(w, groups, n):
    d = w.shape[0]
    parts = [w[:, g * d:(g + 1) * d].reshape(d, n, d // n) for g in groups]
    return jnp.concatenate(parts, axis=2).transpose(1, 0, 2)


def _mixer(x, mixn, win, lbl, hn, wo, cw, cb, lng, lnb, wpw, bpw, wout, *, layer):
    b, l, d = x.shape
    tm = MIX_TM
    n_chunks = tm // CHUNK
    n_conv = tm // CONV_ROWS
    assert (d // n_chunks) % HEAD_DIM == 0 and PAIR % (d // n_conv) == 0
    wqf = win[:, :2 * d]
    wa = _column_slices(win, (4, 5), n_chunks)
    wg = _column_slices(win, (2, 3, 6, 7), n_conv)
    act = lambda: pltpu.VMEM((tm, d), F32)
    act16 = lambda: pltpu.VMEM((tm, d), BF16)
    n_var = 2 * (len(LEVELS) + 1)
    return pl.pallas_call(
        functools.partial(_mixer_body, layer=layer),
        grid=(b, l // tm),
        in_specs=[
            pl.BlockSpec((1, tm, d), lambda i, j: (i, j, 0)),
            _resident(mixn.shape), _resident(wqf.shape), _resident(wa.shape), _resident(wg.shape),
            _resident(lbl.shape), _resident(hn.shape), _resident(wo.shape), _resident(cw.shape),
            _resident(cb.shape), _resident(lng.shape), _resident(lnb.shape), _resident(wpw.shape),
            _resident(bpw.shape), _resident(wout.shape),
        ],
        out_specs=pl.BlockSpec((1, tm, d), lambda i, j: (i, j, 0)),
        out_shape=jax.ShapeDtypeStruct((b, l, d), F32),
        scratch_shapes=[
            pltpu.VMEM((HEADS, HEAD_DIM, HEAD_DIM), F32),
            pltpu.VMEM((TAIL, d), F32),
            pltpu.VMEM((SUB, SUB + TAIL + tm, d), F32),
            pltpu.VMEM((n_chunks * SUB, d), F32),
            act16(),
            pltpu.VMEM(wa.shape[:1] + (tm,) + wa.shape[2:], F32),
            pltpu.VMEM(wg.shape[:1] + (tm,) + wg.shape[2:], F32),
            act(), act(), act(), act(), act(),
        ] + [act16() for _ in range(n_var)],
        compiler_params=pltpu.CompilerParams(
            dimension_semantics=("arbitrary", "arbitrary"), vmem_limit_bytes=VMEM_LIMIT),
        name="mixer",
    )(x, mixn, wqf, wa, wg, lbl, hn, wo, cw, cb, lng, lnb, wpw, bpw, wout)


def kernel(x, ffn1_norm, ffn1_w_gate, ffn1_w_up, ffn1_w_down, mix_norm, w_in, hgrn_lb_logits, hgrn_head_norm, hgrn_w_o, conv_w, conv_b, conv_ln_g, conv_ln_b, conv_w_pw, conv_b_pw, w_out, ffn2_norm, ffn2_w_gate, ffn2_w_up, ffn2_w_down, final_norm):
    b, l, d = x.shape
    depth = ffn1_norm.shape[0]
    row = lambda a: a.reshape(1, -1)
    w16 = lambda a: a.astype(BF16)
    sub8 = lambda a: jnp.broadcast_to(a[..., None, :], a.shape[:-1] + (SUB, a.shape[-1]))
    fin = row(final_norm)
    for i in range(depth):
        x = _ffn(x.reshape(b * l, d), row(ffn1_norm[i]), w16(ffn1_w_gate[i]), w16(ffn1_w_up[i]),
                 w16(ffn1_w_down[i]), fin, final=False).reshape(b, l, d)
        x = _mixer(x, row(mix_norm[i]), w16(w_in[i]), hgrn_lb_logits, row(hgrn_head_norm[i]),
                   w16(hgrn_w_o[i]), sub8(conv_w[i]), sub8(conv_b[i]), row(conv_ln_g[i]),
                   row(conv_ln_b[i]), w16(conv_w_pw[i]), row(conv_b_pw[i]), w16(w_out[i]), layer=i)
        last = i == depth - 1
        x = _ffn(x.reshape(b * l, d), row(ffn2_norm[i]), w16(ffn2_w_gate[i]), w16(ffn2_w_up[i]),
                 w16(ffn2_w_down[i]), fin, final=last).reshape(b, l, d)
    return x
```

```python
import functools

import jax
import jax.numpy as jnp
from jax import lax
from jax.experimental import pallas as pl
from jax.experimental.pallas import tpu as pltpu

D_MODEL = 1024
D_FF = 2816
HEADS = 8
HEAD_DIM = 128
PAIR = 2 * HEAD_DIM
CONV_K = 31
FFN_RES = 0.5
EPS = 1e-6
N_SPLITS = 8

CHUNK = 64
SUB = 8
NBLK = CHUNK // SUB
TAIL = 32
MID = 3
CONV_ROWS = 32
CONV_LANES = 256
LEVELS = ("d", "8", "16", "32")

FFN_TM = 512
MIX_TM = 256
VMEM_LIMIT = 60 * 1024 * 1024

F32 = jnp.float32
BF16 = jnp.bfloat16
NT = (((1,), (1,)), ((), ()))
TN = (((0,), (0,)), ((), ()))


def _rms(x, g):
    ms = jnp.mean(x * x, axis=-1, keepdims=True)
    return x * lax.rsqrt(ms + EPS) * g


def _sigmoid(x):
    return 0.5 * jnp.tanh(0.5 * x) + 0.5


def _block_rows(i, n):
    if isinstance(i, int):
        return pl.ds(i * n, n)
    return pl.ds(pl.multiple_of(i * n, n), n)


def _resident(shape):
    nd = len(shape)
    return pl.BlockSpec(shape, lambda *_: (0,) * nd, pipeline_mode=pl.Buffered(1))


def _ffn_body(x_ref, nrm_ref, wg_ref, wu_ref, wd_ref, fin_ref, o_ref, *, final):
    x = x_ref[...]
    h = _rms(x, nrm_ref[...]).astype(BF16)
    g = jnp.dot(h, wg_ref[...], preferred_element_type=F32)
    u = jnp.dot(h, wu_ref[...], preferred_element_type=F32)
    a = (g * _sigmoid(g) * u).astype(BF16)
    y = x + FFN_RES * jnp.dot(a, wd_ref[...], preferred_element_type=F32)
    if final:
        y = _rms(y, fin_ref[...])
    o_ref[...] = y


def _ffn(x2d, nrm, wg, wu, wd, fin, *, final):
    t, d = x2d.shape
    f = wg.shape[1]
    tm = FFN_TM
    return pl.pallas_call(
        functools.partial(_ffn_body, final=final),
        grid=(t // tm,),
        in_specs=[
            pl.BlockSpec((tm, d), lambda i: (i, 0)),
            _resident((1, d)),
            _resident((d, f)),
            _resident((d, f)),
            _resident((f, d)),
            _resident((1, d)),
        ],
        out_specs=pl.BlockSpec((tm, d), lambda i: (i, 0)),
        out_shape=jax.ShapeDtypeStruct((t, d), F32),
        compiler_params=pltpu.CompilerParams(
            dimension_semantics=("arbitrary",), vmem_limit_bytes=VMEM_LIMIT),
        name="ffn_final" if final else "ffn",
    )(x2d, nrm, wg, wu, wd, fin)


def _hgrn_prep(c, carry, *, q_s, k_s, lf_s, qv, kv, gt_s):
    rows = _block_rows(c, CHUNK)
    lf, q, k = lf_s[rows, :], q_s[rows, :], k_s[rows, :]
    rid = lax.broadcasted_iota(jnp.int32, (SUB, D_MODEL), 0)

    qe, kf, qd, kd, g = [], [], [], [], []
    for i in range(NBLK):
        sl = slice(i * SUB, (i + 1) * SUB)
        a = lf[sl]
        for s in (1, 2, 4):
            a = a + jnp.where(rid >= s, pltpu.roll(a, s, 0), 0.0)
        tot = jnp.broadcast_to(a[SUB - 1:SUB], (SUB, D_MODEL))
        dm = a - jnp.broadcast_to(a[MID:MID + 1], (SUB, D_MODEL))
        qe.append(q[sl] * jnp.exp(a))
        kf.append(k[sl] * jnp.exp(tot - a))
        qd.append(q[sl] * jnp.exp(dm))
        kd.append(k[sl] * jnp.exp(-dm))
        g.append(jnp.exp(tot))

    def running(blocks):
        out, acc = [None], None
        for blk in blocks:
            acc = blk if acc is None else acc * blk
            out.append(acc)
        return out

    def scaled(base, factors):
        return [b if f is None else b * f for b, f in zip(base, factors)]

    def store(ref, blocks):
        ref[rows, :] = jnp.concatenate(blocks, axis=0).astype(BF16)

    def q_factors(nb):
        out = []
        for lo in range(0, NBLK, nb):
            out += running(g[lo:lo + nb])[:nb]
        return out

    def k_factors(nb):
        out = []
        for lo in range(0, NBLK, nb):
            out += running(g[lo:lo + nb][::-1])[:nb][::-1]
        return out

    store(qv["d"], qd)
    store(kv["d"], kd)
    store(qv["8"], qe)
    store(kv["8"], kf)
    for name, nb in (("16", 2), ("32", 4), ("c", NBLK)):
        store(qv[name], scaled(qe, q_factors(nb)))
        store(kv[name], scaled(kf, k_factors(nb)))
    gt_s[_block_rows(c, SUB), :] = running(g)[NBLK]
    return carry


def _hgrn_mm(c, carry, *, qv, kv, v_s, o_s, st_ref, gt_s):
    rows = _block_rows(c, CHUNK)
    gt = gt_s[_block_rows(c, SUB), :]
    v = [v_s[rows, p * PAIR:(p + 1) * PAIR] for p in range(HEADS // 2)]

    ti = lax.broadcasted_iota(jnp.int32, (CHUNK, 2 * CHUNK), 0)
    si = lax.broadcasted_iota(jnp.int32, (CHUNK, 2 * CHUNK), 1) % CHUNK
    bt, bs = ti // SUB, si // SUB
    masks = {
        "d": (bt == bs) & (si <= ti),
        "8": (bt == bs + 1) & (bt % 2 == 1),
        "16": (bt // 4 == bs // 4) & ((bt // 2) % 2 == 1) & ((bs // 2) % 2 == 0),
        "32": (bt // 4 == 1) & (bs // 4 == 0),
    }

    def bdiag(x):
        z = jnp.zeros((x.shape[0], HEAD_DIM), x.dtype)
        return jnp.concatenate(
            [jnp.concatenate([x[:, :HEAD_DIM], z], axis=1),
             jnp.concatenate([z, x[:, HEAD_DIM:]], axis=1)], axis=0)

    pairs = [slice(p * PAIR, (p + 1) * PAIR) for p in range(HEADS // 2)]
    scores = [{name: lax.dot_general(qv[name][rows, ps], bdiag(kv[name][rows, ps]), NT,
                                     preferred_element_type=F32) for name in LEVELS}
              for ps in pairs]
    states = [(st_ref[2 * p], st_ref[2 * p + 1]) for p in range(len(pairs))]
    inter = [lax.dot_general(qv["c"][rows, ps],
                             bdiag(jnp.concatenate(st, axis=1).astype(BF16)), NT,
                             preferred_element_type=F32) for ps, st in zip(pairs, states)]
    upds = [lax.dot_general(v[p], kv["c"][rows, ps], TN, preferred_element_type=F32)
            for p, ps in enumerate(pairs)]
    for p, ps in enumerate(pairs):
        pm = jnp.zeros((CHUNK, 2 * CHUNK), F32)
        for name in LEVELS:
            pm = jnp.where(masks[name], scores[p][name], pm)
        o_s[rows, ps] = inter[p] + jnp.dot(pm.astype(BF16), bdiag(v[p]),
                                           preferred_element_type=F32)
        g_p = jnp.tile(gt[:, ps], (HEAD_DIM // SUB, 1))
        st0, st1 = states[p]
        st_ref[2 * p] = st0 * g_p[:, :HEAD_DIM] + upds[p][:HEAD_DIM, :HEAD_DIM]
        st_ref[2 * p + 1] = st1 * g_p[:, HEAD_DIM:] + upds[p][HEAD_DIM:, HEAD_DIM:]
    return carry


def _conv_block(r, carry, *, ush_ref, cw_ref, cb_ref, y_s, rows_per):
    r0 = pl.multiple_of(r * rows_per, rows_per)
    nsub = rows_per // SUB
    lead = TAIL - CONV_K + 1
    for lo in range(0, D_MODEL, CONV_LANES):
        lanes = slice(lo, lo + CONV_LANES)
        acc = [cb_ref[:, lanes]] * nsub
        for shift in range(SUB):
            taps = [j for j in range(CONV_K) if (lead + j) % SUB == shift]
            w = {j: cw_ref[j, :, lanes] for j in taps}
            first = min((lead + j) // SUB for j in taps)
            last = max((lead + j) // SUB for j in taps) + nsub - 1
            for m in range(first, last + 1):
                start = pl.multiple_of(r0 + (m + 1) * SUB, SUB)
                blk = ush_ref[shift, pl.ds(start, SUB), lanes]
                for j in taps:
                    b = m - (lead + j) // SUB
                    if 0 <= b < nsub:
                        acc[b] = acc[b] + blk * w[j]
        for b in range(nsub):
            y_s[pl.ds(pl.multiple_of(r0 + b * SUB, SUB), SUB), lanes] = acc[b]
    return carry


def _mixer_body(x_ref, mixn_ref, win_ref, lbl_ref, hn_ref, wo_ref, cw_ref, cb_ref, lng_ref,
                lnb_ref, wpw_ref, bpw_ref, wout_ref, o_ref,
                st_ref, tail_ref, ush_ref, gt_s, v_s, q_s, k_s, lf_s, o_s, y_s, *var_refs, layer):
    tm = x_ref.shape[1]
    names = LEVELS + ("c",)
    qv = dict(zip(names, var_refs[:len(names)]))
    kv = dict(zip(names, var_refs[len(names):]))

    @pl.when(pl.program_id(1) == 0)
    def _():
        st_ref[...] = jnp.zeros_like(st_ref)
        tail_ref[...] = jnp.zeros_like(tail_ref)

    x = x_ref[0]
    h = _rms(x, mixn_ref[...]).astype(BF16)

    def proj(i):
        return jnp.dot(h, win_ref[:, i * D_MODEL:(i + 1) * D_MODEL], preferred_element_type=F32)

    lbl = lbl_ref[...]
    e = jnp.exp(lbl - jnp.max(lbl, axis=0, keepdims=True))
    lb = jnp.sum(e[0:layer + 1], axis=0, keepdims=True) / jnp.sum(e, axis=0, keepdims=True)

    u = proj(4) * _sigmoid(proj(5))
    ext = jnp.concatenate([tail_ref[...], u], axis=0)
    for r in range(SUB):
        ush_ref[r, SUB - r:SUB - r + TAIL + tm, :] = ext
    tail_ref[...] = u[tm - TAIL:tm]

    f = lb + (1.0 - lb) * _sigmoid(proj(1))
    f = jnp.clip(f, 1e-6, 1.0)
    lf_s[...] = jnp.log(f)
    k_s[...] = 1.0 - f
    q_s[...] = proj(0)
    v_s[...] = proj(2).astype(BF16)

    n_chunks = tm // CHUNK
    lax.fori_loop(0, n_chunks,
                  functools.partial(_hgrn_prep, q_s=q_s, k_s=k_s, lf_s=lf_s, qv=qv, kv=kv,
                                    gt_s=gt_s), 0)
    lax.fori_loop(0, n_chunks,
                  functools.partial(_hgrn_mm, qv=qv, kv=kv, v_s=v_s, o_s=o_s, st_ref=st_ref,
                                    gt_s=gt_s), 0)
    lax.fori_loop(0, tm // CONV_ROWS,
                  functools.partial(_conv_block, ush_ref=ush_ref, cw_ref=cw_ref, cb_ref=cb_ref,
                                    y_s=y_s, rows_per=CONV_ROWS), 0)

    o = o_s[...]
    hn = hn_ref[...]
    o = jnp.concatenate(
        [_rms(o[:, i * HEAD_DIM:(i + 1) * HEAD_DIM], hn) for i in range(HEADS)], axis=1)
    g_out = proj(3)
    y_a = jnp.dot((o * (g_out * _sigmoid(g_out))).astype(BF16), wo_ref[...],
                  preferred_element_type=F32)

    u = y_s[...]
    mu = jnp.mean(u, axis=-1, keepdims=True)
    uc = u - mu
    var = jnp.mean(uc * uc, axis=-1, keepdims=True)
    u = uc * lax.rsqrt(var + EPS) * lng_ref[...] + lnb_ref[...]
    u = u * _sigmoid(u)
    y_b = jnp.dot(u.astype(BF16), wpw_ref[...], preferred_element_type=F32) + bpw_ref[...]

    merged = _sigmoid(proj(6)) * y_a + _sigmoid(proj(7)) * y_b
    o_ref[0] = x + jnp.dot(merged.astype(BF16), wout_ref[...], preferred_element_type=F32)


def _mixer(x, mixn, win, lbl, hn, wo, cw, cb, lng, lnb, wpw, bpw, wout, *, layer):
    b, l, d = x.shape
    tm = MIX_TM
    act = lambda: pltpu.VMEM((tm, d), F32)
    act16 = lambda: pltpu.VMEM((tm, d), BF16)
    n_var = 2 * (len(LEVELS) + 1)
    return pl.pallas_call(
        functools.partial(_mixer_body, layer=layer),
        grid=(b, l // tm),
        in_specs=[
            pl.BlockSpec((1, tm, d), lambda i, j: (i, j, 0)),
            _resident(mixn.shape), _resident(win.shape), _resident(lbl.shape),
            _resident(hn.shape), _resident(wo.shape), _resident(cw.shape), _resident(cb.shape),
            _resident(lng.shape), _resident(lnb.shape), _resident(wpw.shape),
            _resident(bpw.shape), _resident(wout.shape),
        ],
        out_specs=pl.BlockSpec((1, tm, d), lambda i, j: (i, j, 0)),
        out_shape=jax.ShapeDtypeStruct((b, l, d), F32),
        scratch_shapes=[
            pltpu.VMEM((HEADS, HEAD_DIM, HEAD_DIM), F32),
            pltpu.VMEM((TAIL, d), F32),
            pltpu.VMEM((SUB, SUB + TAIL + tm, d), F32),
            pltpu.VMEM((tm // CHUNK * SUB, d), F32),
            act16(),
            act(), act(), act(), act(), act(),
        ] + [act16() for _ in range(n_var)],
        compiler_params=pltpu.CompilerParams(
            dimension_semantics=("arbitrary", "arbitrary"), vmem_limit_bytes=VMEM_LIMIT),
        name="mixer",
    )(x, mixn, win, lbl, hn, wo, cw, cb, lng, lnb, wpw, bpw, wout)


def kernel(x, ffn1_norm, ffn1_w_gate, ffn1_w_up, ffn1_w_down, mix_norm, w_in, hgrn_lb_logits, hgrn_head_norm, hgrn_w_o, conv_w, conv_b, conv_ln_g, conv_ln_b, conv_w_pw, conv_b_pw, w_out, ffn2_norm, ffn2_w_gate, ffn2_w_up, ffn2_w_down, final_norm):
    b, l, d = x.shape
    depth = ffn1_norm.shape[0]
    row = lambda a: a.reshape(1, -1)
    w16 = lambda a: a.astype(BF16)
    sub8 = lambda a: jnp.broadcast_to(a[..., None, :], a.shape[:-1] + (SUB, a.shape[-1]))
    fin = row(final_norm)
    for i in range(depth):
        x = _ffn(x.reshape(b * l, d), row(ffn1_norm[i]), w16(ffn1_w_gate[i]), w16(ffn1_w_up[i]),
                 w16(ffn1_w_down[i]), fin, final=False).reshape(b, l, d)
        x = _mixer(x, row(mix_norm[i]), w16(w_in[i]), hgrn_lb_logits, row(hgrn_head_norm[i]),
                   w16(hgrn_w_o[i]), sub8(conv_w[i]), sub8(conv_b[i]), row(conv_ln_g[i]),
                   row(conv_ln_b[i]), w16(conv_w_pw[i]), row(conv_b_pw[i]), w16(w_out[i]), layer=i)
        last = i == depth - 1
        x = _ffn(x.reshape(b * l, d), row(ffn2_norm[i]), w16(ffn2_w_gate[i]), w16(ffn2_w_up[i]),
                 w16(ffn2_w_down[i]), fin, final=last).reshape(b, l, d)
    return x
```

```python
import functools

import jax
import jax.numpy as jnp
from jax import lax
from jax.experimental import pallas as pl
from jax.experimental.pallas import tpu as pltpu

D_MODEL = 1024
D_FF = 2816
HEADS = 8
HEAD_DIM = 128
PAIR = 2 * HEAD_DIM
CONV_K = 31
FFN_RES = 0.5
EPS = 1e-6
N_SPLITS = 8

CHUNK = 64
SUB = 8
NBLK = CHUNK // SUB
TAIL = 32
MID = 3
CONV_ROWS = 32
CONV_LANES = 256
LEVELS = ("d", "8", "16", "32")

FFN_TM = 512
MIX_TM = 256
VMEM_LIMIT = 60 * 1024 * 1024

F32 = jnp.float32
BF16 = jnp.bfloat16
NT = (((1,), (1,)), ((), ()))
TN = (((0,), (0,)), ((), ()))


def _rms(x, g):
    ms = jnp.mean(x * x, axis=-1, keepdims=True)
    return x * lax.rsqrt(ms + EPS) * g


def _sigmoid(x):
    return 0.5 * jnp.tanh(0.5 * x) + 0.5


def _aligned(start, n):
    return pl.ds(start if isinstance(start, int) else pl.multiple_of(start, n), n)


def _block_rows(i, n):
    return _aligned(i * n, n)


def _resident(shape):
    nd = len(shape)
    return pl.BlockSpec(shape, lambda *_: (0,) * nd, pipeline_mode=pl.Buffered(1))


def _ffn_body(x_ref, nrm_ref, wg_ref, wu_ref, wd_ref, fin_ref, o_ref, *, final):
    x = x_ref[...]
    h = _rms(x, nrm_ref[...]).astype(BF16)
    g = jnp.dot(h, wg_ref[...], preferred_element_type=F32)
    u = jnp.dot(h, wu_ref[...], preferred_element_type=F32)
    a = (g * _sigmoid(g) * u).astype(BF16)
    y = x + FFN_RES * jnp.dot(a, wd_ref[...], preferred_element_type=F32)
    if final:
        y = _rms(y, fin_ref[...])
    o_ref[...] = y


def _ffn(x2d, nrm, wg, wu, wd, fin, *, final):
    t, d = x2d.shape
    f = wg.shape[1]
    tm = FFN_TM
    return pl.pallas_call(
        functools.partial(_ffn_body, final=final),
        grid=(t // tm,),
        in_specs=[
            pl.BlockSpec((tm, d), lambda i: (i, 0)),
            _resident((1, d)),
            _resident((d, f)),
            _resident((d, f)),
            _resident((f, d)),
            _resident((1, d)),
        ],
        out_specs=pl.BlockSpec((tm, d), lambda i: (i, 0)),
        out_shape=jax.ShapeDtypeStruct((t, d), F32),
        compiler_params=pltpu.CompilerParams(
            dimension_semantics=("arbitrary",), vmem_limit_bytes=VMEM_LIMIT),
        name="ffn_final" if final else "ffn",
    )(x2d, nrm, wg, wu, wd, fin)


def _hgrn_prep(c, carry, *, q_s, k_s, lf_s, qv, kv, gt_s):
    rows = _block_rows(c, CHUNK)
    lf, q, k = lf_s[rows, :], q_s[rows, :], k_s[rows, :]
    rid = lax.broadcasted_iota(jnp.int32, (SUB, D_MODEL), 0)

    qe, kf, qd, kd, g = [], [], [], [], []
    for i in range(NBLK):
        sl = slice(i * SUB, (i + 1) * SUB)
        a = lf[sl]
        for s in (1, 2, 4):
            a = a + jnp.where(rid >= s, pltpu.roll(a, s, 0), 0.0)
        tot = jnp.broadcast_to(a[SUB - 1:SUB], (SUB, D_MODEL))
        dm = a - jnp.broadcast_to(a[MID:MID + 1], (SUB, D_MODEL))
        qe.append(q[sl] * jnp.exp(a))
        kf.append(k[sl] * jnp.exp(tot - a))
        qd.append(q[sl] * jnp.exp(dm))
        kd.append(k[sl] * jnp.exp(-dm))
        g.append(jnp.exp(tot))

    def running(blocks):
        out, acc = [None], None
        for blk in blocks:
            acc = blk if acc is None else acc * blk
            out.append(acc)
        return out

    def scaled(base, factors):
        return [b if f is None else b * f for b, f in zip(base, factors)]

    def store(ref, blocks):
        ref[rows, :] = jnp.concatenate(blocks, axis=0).astype(BF16)

    def q_factors(nb):
        out = []
        for lo in range(0, NBLK, nb):
            out += running(g[lo:lo + nb])[:nb]
        return out

    def k_factors(nb):
        out = []
        for lo in range(0, NBLK, nb):
            out += running(g[lo:lo + nb][::-1])[:nb][::-1]
        return out

    store(qv["d"], qd)
    store(kv["d"], kd)
    store(qv["8"], qe)
    store(kv["8"], kf)
    for name, nb in (("16", 2), ("32", 4), ("c", NBLK)):
        store(qv[name], scaled(qe, q_factors(nb)))
        store(kv[name], scaled(kf, k_factors(nb)))
    gt_s[_block_rows(c, SUB), :] = running(g)[NBLK]
    return carry


def _hgrn_mm(c, carry, *, qv, kv, v_s, o_s, st_ref, gt_s):
    rows = _block_rows(c, CHUNK)
    gt = gt_s[_block_rows(c, SUB), :]
    v = [v_s[rows, p * PAIR:(p + 1) * PAIR] for p in range(HEADS // 2)]

    ti = lax.broadcasted_iota(jnp.int32, (CHUNK, 2 * CHUNK), 0)
    si = lax.broadcasted_iota(jnp.int32, (CHUNK, 2 * CHUNK), 1) % CHUNK
    bt, bs = ti // SUB, si // SUB
    masks = {
        "d": (bt == bs) & (si <= ti),
        "8": (bt == bs + 1) & (bt % 2 == 1),
        "16": (bt // 4 == bs // 4) & ((bt // 2) % 2 == 1) & ((bs // 2) % 2 == 0),
        "32": (bt // 4 == 1) & (bs // 4 == 0),
    }

    def bdiag(x):
        z = jnp.zeros((x.shape[0], HEAD_DIM), x.dtype)
        return jnp.concatenate(
            [jnp.concatenate([x[:, :HEAD_DIM], z], axis=1),
             jnp.concatenate([z, x[:, HEAD_DIM:]], axis=1)], axis=0)

    pairs = [slice(p * PAIR, (p + 1) * PAIR) for p in range(HEADS // 2)]
    scores = [{name: lax.dot_general(qv[name][rows, ps], bdiag(kv[name][rows, ps]), NT,
                                     preferred_element_type=F32) for name in LEVELS}
              for ps in pairs]
    states = [(st_ref[2 * p], st_ref[2 * p + 1]) for p in range(len(pairs))]
    inter = [lax.dot_general(qv["c"][rows, ps],
                             bdiag(jnp.concatenate(st, axis=1).astype(BF16)), NT,
                             preferred_element_type=F32) for ps, st in zip(pairs, states)]
    upds = [lax.dot_general(jnp.concatenate([v[p][:, :HEAD_DIM], v[p][:, HEAD_DIM:]], axis=0),
                            bdiag(kv["c"][rows, ps]), TN, preferred_element_type=F32)
            for p, ps in enumerate(pairs)]
    for p, ps in enumerate(pairs):
        pm = jnp.zeros((CHUNK, 2 * CHUNK), F32)
        for name in LEVELS:
            pm = jnp.where(masks[name], scores[p][name], pm)
        o_s[rows, ps] = inter[p] + jnp.dot(pm.astype(BF16), bdiag(v[p]),
                                           preferred_element_type=F32)
        g_p = jnp.tile(gt[:, ps], (HEAD_DIM // SUB, 1))
        st0, st1 = states[p]
        st_ref[2 * p] = st0 * g_p[:, :HEAD_DIM] + upds[p][:, :HEAD_DIM]
        st_ref[2 * p + 1] = st1 * g_p[:, HEAD_DIM:] + upds[p][:, HEAD_DIM:]
    return carry


def _conv_block(r, carry, *, ush_ref, cw_ref, cb_ref, y_s, rows_per):
    r0 = r * rows_per
    nsub = rows_per // SUB
    lead = TAIL - CONV_K + 1
    for lo in range(0, D_MODEL, CONV_LANES):
        lanes = slice(lo, lo + CONV_LANES)
        acc = [cb_ref[:, lanes]] * nsub
        for shift in range(SUB):
            taps = [j for j in range(CONV_K) if (lead + j) % SUB == shift]
            w = {j: cw_ref[j, :, lanes] for j in taps}
            first = min((lead + j) // SUB for j in taps)
            last = max((lead + j) // SUB for j in taps) + nsub - 1
            for m in range(first, last + 1):
                blk = ush_ref[shift, _aligned(r0 + (m + 1) * SUB, SUB), lanes]
                for j in taps:
                    b = m - (lead + j) // SUB
                    if 0 <= b < nsub:
                        acc[b] = acc[b] + blk * w[j]
        for b in range(nsub):
            y_s[_aligned(r0 + b * SUB, SUB), lanes] = acc[b]
    return carry


def _mixer_body(x_ref, mixn_ref, win_ref, lbl_ref, hn_ref, wo_ref, cw_ref, cb_ref, lng_ref,
                lnb_ref, wpw_ref, bpw_ref, wout_ref, o_ref,
                st_ref, tail_ref, ush_ref, gt_s, h_s, v_s, q_s, k_s, lf_s, o_s, y_s, *var_refs, layer):
    tm = x_ref.shape[1]
    names = LEVELS + ("c",)
    qv = dict(zip(names, var_refs[:len(names)]))
    kv = dict(zip(names, var_refs[len(names):]))

    @pl.when(pl.program_id(1) == 0)
    def _():
        st_ref[...] = jnp.zeros_like(st_ref)
        tail_ref[...] = jnp.zeros_like(tail_ref)

    x = x_ref[0]
    h_s[...] = _rms(x, mixn_ref[...]).astype(BF16)

    def proj(i):
        return jnp.dot(h_s[...], win_ref[:, i * D_MODEL:(i + 1) * D_MODEL],
                       preferred_element_type=F32)

    lbl = lbl_ref[...]
    e = jnp.exp(lbl - jnp.max(lbl, axis=0, keepdims=True))
    lb = jnp.sum(e[0:layer + 1], axis=0, keepdims=True) / jnp.sum(e, axis=0, keepdims=True)

    u = proj(4) * _sigmoid(proj(5))
    ext = jnp.concatenate([tail_ref[...], u], axis=0)
    n_ext = TAIL + tm
    tiles = [ext[i:i + SUB] for i in range(0, n_ext, SUB)]
    rid = lax.broadcasted_iota(jnp.int32, (SUB, D_MODEL), 0)
    ush_ref[0, SUB:SUB + n_ext, :] = ext
    rot = tiles
    for r in range(SUB - 1, 0, -1):
        rot = [pltpu.roll(t, 1, 0) for t in rot]
        ush_ref[r, SUB:SUB + n_ext, :] = jnp.concatenate(
            [jnp.where(rid < SUB - r, rot[i], rot[(i + 1) % len(rot)]) for i in range(len(rot))],
            axis=0)
    tail_ref[...] = u[tm - TAIL:tm]

    f = lb + (1.0 - lb) * _sigmoid(proj(1))
    f = jnp.clip(f, 1e-6, 1.0)
    lf_s[...] = jnp.log(f)
    k_s[...] = 1.0 - f
    q_s[...] = proj(0)
    v_s[...] = proj(2).astype(BF16)

    n_chunks = tm // CHUNK
    for c in range(n_chunks):
        if c == 0:
            _hgrn_prep(0, 0, q_s=q_s, k_s=k_s, lf_s=lf_s, qv=qv, kv=kv, gt_s=gt_s)
        _hgrn_mm(c, 0, qv=qv, kv=kv, v_s=v_s, o_s=o_s, st_ref=st_ref, gt_s=gt_s)
        if c + 1 < n_chunks:
            _hgrn_prep(c + 1, 0, q_s=q_s, k_s=k_s, lf_s=lf_s, qv=qv, kv=kv, gt_s=gt_s)
    lax.fori_loop(0, tm // CONV_ROWS,
                  functools.partial(_conv_block, ush_ref=ush_ref, cw_ref=cw_ref, cb_ref=cb_ref,
                                    y_s=y_s, rows_per=CONV_ROWS), 0)

    o = o_s[...]
    hn = hn_ref[...]
    o = jnp.concatenate(
        [_rms(o[:, i * HEAD_DIM:(i + 1) * HEAD_DIM], hn) for i in range(HEADS)], axis=1)
    g_out = proj(3)
    y_a = jnp.dot((o * (g_out * _sigmoid(g_out))).astype(BF16), wo_ref[...],
                  preferred_element_type=F32)

    u = y_s[...]
    mu = jnp.mean(u, axis=-1, keepdims=True)
    uc = u - mu
    var = jnp.mean(uc * uc, axis=-1, keepdims=True)
    u = uc * lax.rsqrt(var + EPS) * lng_ref[...] + lnb_ref[...]
    u = u * _sigmoid(u)
    y_b = jnp.dot(u.astype(BF16), wpw_ref[...], preferred_element_type=F32) + bpw_ref[...]

    merged = _sigmoid(proj(6)) * y_a + _sigmoid(proj(7)) * y_b
    o_ref[0] = x + jnp.dot(merged.astype(BF16), wout_ref[...], preferred_element_type=F32)


def _mixer(x, mixn, win, lbl, hn, wo, cw, cb, lng, lnb, wpw, bpw, wout, *, layer):
    b, l, d = x.shape
    tm = MIX_TM
    act = lambda: pltpu.VMEM((tm, d), F32)
    act16 = lambda: pltpu.VMEM((tm, d), BF16)
    n_var = 2 * (len(LEVELS) + 1)
    return pl.pallas_call(
        functools.partial(_mixer_body, layer=layer),
        grid=(b, l // tm),
        in_specs=[
            pl.BlockSpec((1, tm, d), lambda i, j: (i, j, 0)),
            _resident(mixn.shape), _resident(win.shape), _resident(lbl.shape),
            _resident(hn.shape), _resident(wo.shape), _resident(cw.shape), _resident(cb.shape),
            _resident(lng.shape), _resident(lnb.shape), _resident(wpw.shape),
            _resident(bpw.shape), _resident(wout.shape),
        ],
        out_specs=pl.BlockSpec((1, tm, d), lambda i, j: (i, j, 0)),
        out_shape=jax.ShapeDtypeStruct((b, l, d), F32),
        scratch_shapes=[
            pltpu.VMEM((HEADS, HEAD_DIM, HEAD_DIM), F32),
            pltpu.VMEM((TAIL, d), F32),
            pltpu.VMEM((SUB, SUB + TAIL + tm, d), F32),
            pltpu.VMEM((tm // CHUNK * SUB, d), F32),
            act16(),
            act16(),
            act(), act(), act(), act(), act(),
        ] + [act16() for _ in range(n_var)],
        compiler_params=pltpu.CompilerParams(
            dimension_semantics=("arbitrary", "arbitrary"), vmem_limit_bytes=VMEM_LIMIT),
        name="mixer",
    )(x, mixn, win, lbl, hn, wo, cw, cb, lng, lnb, wpw, bpw, wout)


def kernel(x, ffn1_norm, ffn1_w_gate, ffn1_w_up, ffn1_w_down, mix_norm, w_in, hgrn_lb_logits, hgrn_head_norm, hgrn_w_o, conv_w, conv_b, conv_ln_g, conv_ln_b, conv_w_pw, conv_b_pw, w_out, ffn2_norm, ffn2_w_gate, ffn2_w_up, ffn2_w_down, final_norm):
    b, l, d = x.shape
    depth = ffn1_norm.shape[0]
    row = lambda a: a.reshape(1, -1)
    w16 = lambda a: a.astype(BF16)
    sub8 = lambda a: jnp.broadcast_to(a[..., None, :], a.shape[:-1] + (SUB, a.shape[-1]))
    fin = row(final_norm)
    for i in range(depth):
        x = _ffn(x.reshape(b * l, d), row(ffn1_norm[i]), w16(ffn1_w_gate[i]), w16(ffn1_w_up[i]),
                 w16(ffn1_w_down[i]), fin, final=False).reshape(b, l, d)
        x = _mixer(x, row(mix_norm[i]), w16(w_in[i]), hgrn_lb_logits, row(hgrn_head_norm[i]),
                   w16(hgrn_w_o[i]), sub8(conv_w[i]), sub8(conv_b[i]), row(conv_ln_g[i]),
                   row(conv_ln_b[i]), w16(conv_w_pw[i]), row(conv_b_pw[i]), w16(w_out[i]), layer=i)
        last = i == depth - 1
        x = _ffn(x.reshape(b * l, d), row(ffn2_norm[i]), w16(ffn2_w_gate[i]), w16(ffn2_w_up[i]),
                 w16(ffn2_w_down[i]), fin, final=last).reshape(b, l, d)
    return x
```

```python
import functools

import jax
import jax.numpy as jnp
from jax import lax
from jax.experimental import pallas as pl
from jax.experimental.pallas import tpu as pltpu

D_MODEL = 1024
D_FF = 2816
HEADS = 8
HEAD_DIM = 128
PAIR = 2 * HEAD_DIM
CONV_K = 31
FFN_RES = 0.5
EPS = 1e-6
N_SPLITS = 8

CHUNK = 64
SUB = 8
NBLK = CHUNK // SUB
TAIL = 32
MID = 3
CONV_ROWS = 128
CONV_LANES = 256
LEVELS = ("d", "8", "16", "32")

FFN_TM = 512
MIX_TM = 256
VMEM_LIMIT = 60 * 1024 * 1024

F32 = jnp.float32
BF16 = jnp.bfloat16
NT = (((1,), (1,)), ((), ()))
TN = (((0,), (0,)), ((), ()))


def _rms(x, g):
    ms = jnp.mean(x * x, axis=-1, keepdims=True)
    return x * lax.rsqrt(ms + EPS) * g


def _sigmoid(x):
    return 0.5 * jnp.tanh(0.5 * x) + 0.5


def _aligned(start, n):
    return pl.ds(start if isinstance(start, int) else pl.multiple_of(start, n), n)


def _block_rows(i, n):
    return _aligned(i * n, n)


def _resident(shape):
    nd = len(shape)
    return pl.BlockSpec(shape, lambda *_: (0,) * nd, pipeline_mode=pl.Buffered(1))


def _ffn_body(x_ref, nrm_ref, wg_ref, wu_ref, wd_ref, fin_ref, o_ref, *, final):
    x = x_ref[...]
    h = _rms(x, nrm_ref[...]).astype(BF16)
    g = jnp.dot(h, wg_ref[...], preferred_element_type=F32)
    u = jnp.dot(h, wu_ref[...], preferred_element_type=F32)
    a = (g * _sigmoid(g) * u).astype(BF16)
    y = x + FFN_RES * jnp.dot(a, wd_ref[...], preferred_element_type=F32)
    if final:
        y = _rms(y, fin_ref[...])
    o_ref[...] = y


def _ffn(x2d, nrm, wg, wu, wd, fin, *, final):
    t, d = x2d.shape
    f = wg.shape[1]
    tm = FFN_TM
    return pl.pallas_call(
        functools.partial(_ffn_body, final=final),
        grid=(t // tm,),
        in_specs=[
            pl.BlockSpec((tm, d), lambda i: (i, 0)),
            _resident((1, d)),
            _resident((d, f)),
            _resident((d, f)),
            _resident((f, d)),
            _resident((1, d)),
        ],
        out_specs=pl.BlockSpec((tm, d), lambda i: (i, 0)),
        out_shape=jax.ShapeDtypeStruct((t, d), F32),
        compiler_params=pltpu.CompilerParams(
            dimension_semantics=("arbitrary",), vmem_limit_bytes=VMEM_LIMIT),
        name="ffn_final" if final else "ffn",
    )(x2d, nrm, wg, wu, wd, fin)


def _hgrn_prep(c, carry, *, q_s, k_s, lf_s, qv, kv, gt_s):
    rows = _block_rows(c, CHUNK)
    lf, q, k = lf_s[rows, :], q_s[rows, :], k_s[rows, :]
    rid = lax.broadcasted_iota(jnp.int32, (SUB, D_MODEL), 0)

    qe, kf, qd, kd, g = [], [], [], [], []
    for i in range(NBLK):
        sl = slice(i * SUB, (i + 1) * SUB)
        a = lf[sl]
        for s in (1, 2, 4):
            a = a + jnp.where(rid >= s, pltpu.roll(a, s, 0), 0.0)
        tot = jnp.broadcast_to(a[SUB - 1:SUB], (SUB, D_MODEL))
        dm = a - jnp.broadcast_to(a[MID:MID + 1], (SUB, D_MODEL))
        qe.append(q[sl] * jnp.exp(a))
        kf.append(k[sl] * jnp.exp(tot - a))
        qd.append(q[sl] * jnp.exp(dm))
        kd.append(k[sl] * jnp.exp(-dm))
        g.append(jnp.exp(tot))

    def running(blocks):
        out, acc = [None], None
        for blk in blocks:
            acc = blk if acc is None else acc * blk
            out.append(acc)
        return out

    def scaled(base, factors):
        return [b if f is None else b * f for b, f in zip(base, factors)]

    def store(ref, blocks):
        ref[rows, :] = jnp.concatenate(blocks, axis=0).astype(BF16)

    def q_factors(nb):
        out = []
        for lo in range(0, NBLK, nb):
            out += running(g[lo:lo + nb])[:nb]
        return out

    def k_factors(nb):
        out = []
        for lo in range(0, NBLK, nb):
            out += running(g[lo:lo + nb][::-1])[:nb][::-1]
        return out

    store(qv["d"], qd)
    store(kv["d"], kd)
    store(qv["8"], qe)
    store(kv["8"], kf)
    for name, nb in (("16", 2), ("32", 4), ("c", NBLK)):
        store(qv[name], scaled(qe, q_factors(nb)))
        store(kv[name], scaled(kf, k_factors(nb)))
    gt_s[_block_rows(c, SUB), :] = running(g)[NBLK]
    return carry


def _hgrn_mm(c, carry, *, qv, kv, v_s, o_s, st_ref, sn_ref, gt_s):
    rows = _block_rows(c, CHUNK)
    gt = gt_s[_block_rows(c, SUB), :]
    v = [v_s[rows, p * PAIR:(p + 1) * PAIR] for p in range(HEADS // 2)]

    ti = lax.broadcasted_iota(jnp.int32, (CHUNK, 2 * CHUNK), 0)
    si = lax.broadcasted_iota(jnp.int32, (CHUNK, 2 * CHUNK), 1) % CHUNK
    bt, bs = ti // SUB, si // SUB
    masks = {
        "d": (bt == bs) & (si <= ti),
        "8": (bt == bs + 1) & (bt % 2 == 1),
        "16": (bt // 4 == bs // 4) & ((bt // 2) % 2 == 1) & ((bs // 2) % 2 == 0),
        "32": (bt // 4 == 1) & (bs // 4 == 0),
    }

    def bdiag(x):
        z = jnp.zeros((x.shape[0], HEAD_DIM), x.dtype)
        return jnp.concatenate(
            [jnp.concatenate([x[:, :HEAD_DIM], z], axis=1),
             jnp.concatenate([z, x[:, HEAD_DIM:]], axis=1)], axis=0)

    pairs = [slice(p * PAIR, (p + 1) * PAIR) for p in range(HEADS // 2)]
    scores = [{name: lax.dot_general(qv[name][rows, ps], bdiag(kv[name][rows, ps]), NT,
                                     preferred_element_type=F32) for name in LEVELS}
              for ps in pairs]
    states = [(st_ref[2 * p], st_ref[2 * p + 1]) for p in range(len(pairs))]
    inter = [jnp.dot(qv["c"][rows, ps], sn_ref[p], preferred_element_type=F32)
             for p, ps in enumerate(pairs)]
    upds = [lax.dot_general(jnp.concatenate([v[p][:, :HEAD_DIM], v[p][:, HEAD_DIM:]], axis=0),
                            bdiag(kv["c"][rows, ps]), TN, preferred_element_type=F32)
            for p, ps in enumerate(pairs)]
    for p, ps in enumerate(pairs):
        pm = jnp.zeros((CHUNK, 2 * CHUNK), F32)
        for name in LEVELS:
            pm = jnp.where(masks[name], scores[p][name], pm)
        o_s[rows, ps] = inter[p] + jnp.dot(pm.astype(BF16), bdiag(v[p]),
                                           preferred_element_type=F32)
        g_p = jnp.tile(gt[:, ps], (HEAD_DIM // SUB, 1))
        st0, st1 = states[p]
        st0 = st0 * g_p[:, :HEAD_DIM] + upds[p][:, :HEAD_DIM]
        st1 = st1 * g_p[:, HEAD_DIM:] + upds[p][:, HEAD_DIM:]
        st_ref[2 * p], st_ref[2 * p + 1] = st0, st1
        sn_ref[p, :HEAD_DIM, :HEAD_DIM] = st0.T.astype(BF16)
        sn_ref[p, HEAD_DIM:, HEAD_DIM:] = st1.T.astype(BF16)
    return carry


def _conv_block(r, carry, *, ush_ref, cw_ref, cb_ref, y_s, rows_per):
    r0 = r * rows_per
    nsub = rows_per // SUB
    lead = TAIL - CONV_K + 1
    for lo in range(0, D_MODEL, CONV_LANES):
        lanes = slice(lo, lo + CONV_LANES)
        acc = [cb_ref[:, lanes]] * nsub
        for shift in range(SUB):
            taps = [j for j in range(CONV_K) if (lead + j) % SUB == shift]
            w = {j: cw_ref[j, :, lanes] for j in taps}
            first = min((lead + j) // SUB for j in taps)
            last = max((lead + j) // SUB for j in taps) + nsub - 1
            for m in range(first, last + 1):
                blk = ush_ref[shift, _aligned(r0 + (m + 1) * SUB, SUB), lanes]
                for j in taps:
                    b = m - (lead + j) // SUB
                    if 0 <= b < nsub:
                        acc[b] = acc[b] + blk * w[j]
        for b in range(nsub):
            y_s[_aligned(r0 + b * SUB, SUB), lanes] = acc[b]
    return carry


def _mixer_body(x_ref, mixn_ref, win_ref, lbl_ref, hn_ref, wo_ref, cw_ref, cb_ref, lng_ref,
                lnb_ref, wpw_ref, bpw_ref, wout_ref, o_ref,
                st_ref, sn_ref, tail_ref, ush_ref, gt_s, h_s, v_s, q_s, k_s, lf_s, o_s, y_s, *var_refs, layer):
    tm = x_ref.shape[1]
    names = LEVELS + ("c",)
    qv = dict(zip(names, var_refs[:len(names)]))
    kv = dict(zip(names, var_refs[len(names):]))

    @pl.when(pl.program_id(1) == 0)
    def _():
        st_ref[...] = jnp.zeros_like(st_ref)
        sn_ref[...] = jnp.zeros_like(sn_ref)
        tail_ref[...] = jnp.zeros_like(tail_ref)

    x = x_ref[0]
    h_s[...] = _rms(x, mixn_ref[...]).astype(BF16)

    def proj(i):
        return jnp.dot(h_s[...], win_ref[:, i * D_MODEL:(i + 1) * D_MODEL],
                       preferred_element_type=F32)

    lbl = lbl_ref[...]
    e = jnp.exp(lbl - jnp.max(lbl, axis=0, keepdims=True))
    lb = jnp.sum(e[0:layer + 1], axis=0, keepdims=True) / jnp.sum(e, axis=0, keepdims=True)

    u = proj(4) * _sigmoid(proj(5))
    ext = jnp.concatenate([tail_ref[...], u], axis=0)
    n_ext = TAIL + tm
    tiles = [ext[i:i + SUB] for i in range(0, n_ext, SUB)]
    rid = lax.broadcasted_iota(jnp.int32, (SUB, D_MODEL), 0)
    ush_ref[0, SUB:SUB + n_ext, :] = ext
    rot = tiles
    for r in range(SUB - 1, 0, -1):
        rot = [pltpu.roll(t, 1, 0) for t in rot]
        ush_ref[r, SUB:SUB + n_ext, :] = jnp.concatenate(
            [jnp.where(rid < SUB - r, rot[i], rot[(i + 1) % len(rot)]) for i in range(len(rot))],
            axis=0)
    tail_ref[...] = u[tm - TAIL:tm]

    f = lb + (1.0 - lb) * _sigmoid(proj(1))
    f = jnp.clip(f, 1e-6, 1.0)
    lf_s[...] = jnp.log(f)
    k_s[...] = 1.0 - f
    q_s[...] = proj(0)
    v_s[...] = proj(2).astype(BF16)

    n_chunks = tm // CHUNK
    for c in range(n_chunks):
        if c == 0:
            _hgrn_prep(0, 0, q_s=q_s, k_s=k_s, lf_s=lf_s, qv=qv, kv=kv, gt_s=gt_s)
        _hgrn_mm(c, 0, qv=qv, kv=kv, v_s=v_s, o_s=o_s, st_ref=st_ref, sn_ref=sn_ref, gt_s=gt_s)
        if c + 1 < n_chunks:
            _hgrn_prep(c + 1, 0, q_s=q_s, k_s=k_s, lf_s=lf_s, qv=qv, kv=kv, gt_s=gt_s)
    lax.fori_loop(0, tm // CONV_ROWS,
                  functools.partial(_conv_block, ush_ref=ush_ref, cw_ref=cw_ref, cb_ref=cb_ref,
                                    y_s=y_s, rows_per=CONV_ROWS), 0)

    o = o_s[...]
    hn = hn_ref[...]
    o = jnp.concatenate(
        [_rms(o[:, i * HEAD_DIM:(i + 1) * HEAD_DIM], hn) for i in range(HEADS)], axis=1)
    g_out = proj(3)
    y_a = jnp.dot((o * (g_out * _sigmoid(g_out))).astype(BF16), wo_ref[...],
                  preferred_element_type=F32)

    u = y_s[...]
    mu = jnp.mean(u, axis=-1, keepdims=True)
    uc = u - mu
    var = jnp.mean(uc * uc, axis=-1, keepdims=True)
    u = uc * lax.rsqrt(var + EPS) * lng_ref[...] + lnb_ref[...]
    u = u * _sigmoid(u)
    y_b = jnp.dot(u.astype(BF16), wpw_ref[...], preferred_element_type=F32) + bpw_ref[...]

    merged = _sigmoid(proj(6)) * y_a + _sigmoid(proj(7)) * y_b
    o_ref[0] = x + jnp.dot(merged.astype(BF16), wout_ref[...], preferred_element_type=F32)


def _mixer(x, mixn, win, lbl, hn, wo, cw, cb, lng, lnb, wpw, bpw, wout, *, layer):
    b, l, d = x.shape
    tm = MIX_TM
    act = lambda: pltpu.VMEM((tm, d), F32)
    act16 = lambda: pltpu.VMEM((tm, d), BF16)
    n_var = 2 * (len(LEVELS) + 1)
    return pl.pallas_call(
        functools.partial(_mixer_body, layer=layer),
        grid=(b, l // tm),
        in_specs=[
            pl.BlockSpec((1, tm, d), lambda i, j: (i, j, 0)),
            _resident(mixn.shape), _resident(win.shape), _resident(lbl.shape),
            _resident(hn.shape), _resident(wo.shape), _resident(cw.shape), _resident(cb.shape),
            _resident(lng.shape), _resident(lnb.shape), _resident(wpw.shape),
            _resident(bpw.shape), _resident(wout.shape),
        ],
        out_specs=pl.BlockSpec((1, tm, d), lambda i, j: (i, j, 0)),
        out_shape=jax.ShapeDtypeStruct((b, l, d), F32),
        scratch_shapes=[
            pltpu.VMEM((HEADS, HEAD_DIM, HEAD_DIM), F32),
            pltpu.VMEM((HEADS // 2, PAIR, PAIR), BF16),
            pltpu.VMEM((TAIL, d), F32),
            pltpu.VMEM((SUB, SUB + TAIL + tm, d), F32),
            pltpu.VMEM((tm // CHUNK * SUB, d), F32),
            act16(),
            act16(),
            act(), act(), act(), act(), act(),
        ] + [act16() for _ in range(n_var)],
        compiler_params=pltpu.CompilerParams(
            dimension_semantics=("arbitrary", "arbitrary"), vmem_limit_bytes=VMEM_LIMIT),
        name="mixer",
    )(x, mixn, win, lbl, hn, wo, cw, cb, lng, lnb, wpw, bpw, wout)


def kernel(x, ffn1_norm, ffn1_w_gate, ffn1_w_up, ffn1_w_down, mix_norm, w_in, hgrn_lb_logits, hgrn_head_norm, hgrn_w_o, conv_w, conv_b, conv_ln_g, conv_ln_b, conv_w_pw, conv_b_pw, w_out, ffn2_norm, ffn2_w_gate, ffn2_w_up, ffn2_w_down, final_norm):
    b, l, d = x.shape
    depth = ffn1_norm.shape[0]
    row = lambda a: a.reshape(1, -1)
    w16 = lambda a: a.astype(BF16)
    sub8 = lambda a: jnp.broadcast_to(a[..., None, :], a.shape[:-1] + (SUB, a.shape[-1]))
    fin = row(final_norm)
    for i in range(depth):
        x = _ffn(x.reshape(b * l, d), row(ffn1_norm[i]), w16(ffn1_w_gate[i]), w16(ffn1_w_up[i]),
                 w16(ffn1_w_down[i]), fin, final=False).reshape(b, l, d)
        x = _mixer(x, row(mix_norm[i]), w16(w_in[i]), hgrn_lb_logits, row(hgrn_head_norm[i]),
                   w16(hgrn_w_o[i]), sub8(conv_w[i]), sub8(conv_b[i]), row(conv_ln_g[i]),
                   row(conv_ln_b[i]), w16(conv_w_pw[i]), row(conv_b_pw[i]), w16(w_out[i]), layer=i)
        last = i == depth - 1
        x = _ffn(x.reshape(b * l, d), row(ffn2_norm[i]), w16(ffn2_w_gate[i]), w16(ffn2_w_up[i]),
                 w16(ffn2_w_down[i]), fin, final=last).reshape(b, l, d)
    return x
```

```python
import functools

import jax
import jax.numpy as jnp
from jax import lax
from jax.experimental import pallas as pl
from jax.experimental.pallas import tpu as pltpu

D_MODEL = 1024
D_FF = 2816
HEADS = 8
HEAD_DIM = 128
PAIR = 2 * HEAD_DIM
CONV_K = 31
FFN_RES = 0.5
EPS = 1e-6
N_SPLITS = 8

CHUNK = 64
SUB = 8
NBLK = CHUNK // SUB
TAIL = 32
MID = 3
CONV_ROWS = 128
CONV_LANES = 256
LEVELS = ("d", "8", "16", "32")

FFN_TM = 512
FFN_COLS = 1536
MIX_TM = 256
VMEM_LIMIT = 60 * 1024 * 1024

F32 = jnp.float32
BF16 = jnp.bfloat16
NT = (((1,), (1,)), ((), ()))
TN = (((0,), (0,)), ((), ()))


def _rms(x, g):
    ms = jnp.mean(x * x, axis=-1, keepdims=True)
    return x * lax.rsqrt(ms + EPS) * g


def _sigmoid(x):
    return 0.5 * jnp.tanh(0.5 * x) + 0.5


def _aligned(start, n):
    return pl.ds(start if isinstance(start, int) else pl.multiple_of(start, n), n)


def _block_rows(i, n):
    return _aligned(i * n, n)


def _resident(shape):
    nd = len(shape)
    return pl.BlockSpec(shape, lambda *_: (0,) * nd, pipeline_mode=pl.Buffered(1))


def _ffn_body(x_ref, nrm_ref, wg_ref, wu_ref, wd_ref, fin_ref, o_ref, *, final):
    x = x_ref[...]
    h = _rms(x, nrm_ref[...])
    y = x
    f = wg_ref.shape[1]
    for lo in range(0, f, FFN_COLS):
        cols = slice(lo, min(lo + FFN_COLS, f))
        g = jnp.dot(h, wg_ref[:, cols], preferred_element_type=F32)
        u = jnp.dot(h, wu_ref[:, cols], preferred_element_type=F32)
        y = y + FFN_RES * jnp.dot(g * _sigmoid(g) * u, wd_ref[cols, :],
                                  preferred_element_type=F32)
    if final:
        y = _rms(y, fin_ref[...])
    o_ref[...] = y


def _ffn(x2d, nrm, wg, wu, wd, fin, *, layer, final):
    t, d = x2d.shape
    f = wg.shape[2]
    tm = FFN_TM
    weight = lambda r, c: pl.BlockSpec((None, r, c), lambda i: (layer, 0, 0),
                                       pipeline_mode=pl.Buffered(1))
    return pl.pallas_call(
        functools.partial(_ffn_body, final=final),
        grid=(t // tm,),
        in_specs=[
            pl.BlockSpec((tm, d), lambda i: (i, 0)),
            _resident((1, d)),
            weight(d, f),
            weight(d, f),
            weight(f, d),
            _resident((1, d)),
        ],
        out_specs=pl.BlockSpec((tm, d), lambda i: (i, 0)),
        out_shape=jax.ShapeDtypeStruct((t, d), F32),
        compiler_params=pltpu.CompilerParams(
            dimension_semantics=("arbitrary",), vmem_limit_bytes=VMEM_LIMIT),
        name="ffn_final" if final else "ffn",
    )(x2d, nrm, wg, wu, wd, fin)


def _hgrn_prep(c, carry, *, q_s, k_s, lf_s, qv, kv, gt_s):
    rows = _block_rows(c, CHUNK)
    lf, q, k = lf_s[rows, :], q_s[rows, :], k_s[rows, :]
    rid = lax.broadcasted_iota(jnp.int32, (SUB, D_MODEL), 0)

    qe, kf, qd, kd, g = [], [], [], [], []
    for i in range(NBLK):
        sl = slice(i * SUB, (i + 1) * SUB)
        a = lf[sl]
        for s in (1, 2, 4):
            a = a + jnp.where(rid >= s, pltpu.roll(a, s, 0), 0.0)
        tot = jnp.broadcast_to(a[SUB - 1:SUB], (SUB, D_MODEL))
        dm = a - jnp.broadcast_to(a[MID:MID + 1], (SUB, D_MODEL))
        qe.append(q[sl] * jnp.exp(a))
        kf.append(k[sl] * jnp.exp(tot - a))
        qd.append(q[sl] * jnp.exp(dm))
        kd.append(k[sl] * jnp.exp(-dm))
        g.append(jnp.exp(tot))

    def running(blocks):
        out, acc = [None], None
        for blk in blocks:
            acc = blk if acc is None else acc * blk
            out.append(acc)
        return out

    def scaled(base, factors):
        return [b if f is None else b * f for b, f in zip(base, factors)]

    def store(ref, blocks):
        ref[rows, :] = jnp.concatenate(blocks, axis=0).astype(BF16)

    def q_factors(nb):
        out = []
        for lo in range(0, NBLK, nb):
            out += running(g[lo:lo + nb])[:nb]
        return out

    def k_factors(nb):
        out = []
        for lo in range(0, NBLK, nb):
            out += running(g[lo:lo + nb][::-1])[:nb][::-1]
        return out

    store(qv["d"], qd)
    store(kv["d"], kd)
    store(qv["8"], qe)
    store(kv["8"], kf)
    for name, nb in (("16", 2), ("32", 4), ("c", NBLK)):
        store(qv[name], scaled(qe, q_factors(nb)))
        store(kv[name], scaled(kf, k_factors(nb)))
    gt_s[_block_rows(c, SUB), :] = running(g)[NBLK]
    return carry


def _hgrn_mm(c, carry, *, qv, kv, v_s, o_s, st_ref, sn_ref, gt_s):
    rows = _block_rows(c, CHUNK)
    gt = gt_s[_block_rows(c, SUB), :]
    v = [v_s[rows, p * PAIR:(p + 1) * PAIR] for p in range(HEADS // 2)]

    ti = lax.broadcasted_iota(jnp.int32, (CHUNK, 2 * CHUNK), 0)
    si = lax.broadcasted_iota(jnp.int32, (CHUNK, 2 * CHUNK), 1) % CHUNK
    bt, bs = ti // SUB, si // SUB
    masks = {
        "d": (bt == bs) & (si <= ti),
        "8": (bt == bs + 1) & (bt % 2 == 1),
        "16": (bt // 4 == bs // 4) & ((bt // 2) % 2 == 1) & ((bs // 2) % 2 == 0),
        "32": (bt // 4 == 1) & (bs // 4 == 0),
    }

    def bdiag(x):
        z = jnp.zeros((x.shape[0], HEAD_DIM), x.dtype)
        return jnp.concatenate(
            [jnp.concatenate([x[:, :HEAD_DIM], z], axis=1),
             jnp.concatenate([z, x[:, HEAD_DIM:]], axis=1)], axis=0)

    pairs = [slice(p * PAIR, (p + 1) * PAIR) for p in range(HEADS // 2)]
    scores = [{name: lax.dot_general(qv[name][rows, ps], bdiag(kv[name][rows, ps]), NT,
                                     preferred_element_type=F32) for name in LEVELS}
              for ps in pairs]
    states = [(st_ref[2 * p], st_ref[2 * p + 1]) for p in range(len(pairs))]
    inter = [jnp.dot(qv["c"][rows, ps], sn_ref[p], preferred_element_type=F32)
             for p, ps in enumerate(pairs)]
    upds = [lax.dot_general(jnp.concatenate([v[p][:, :HEAD_DIM], v[p][:, HEAD_DIM:]], axis=0),
                            bdiag(kv["c"][rows, ps]), TN, preferred_element_type=F32)
            for p, ps in enumerate(pairs)]
    for p, ps in enumerate(pairs):
        pm = jnp.zeros((CHUNK, 2 * CHUNK), F32)
        for name in LEVELS:
            pm = jnp.where(masks[name], scores[p][name], pm)
        o_s[rows, ps] = inter[p] + jnp.dot(pm.astype(BF16), bdiag(v[p]),
                                           preferred_element_type=F32)
        g_p = jnp.tile(gt[:, ps], (HEAD_DIM // SUB, 1))
        st0, st1 = states[p]
        st0 = st0 * g_p[:, :HEAD_DIM] + upds[p][:, :HEAD_DIM]
        st1 = st1 * g_p[:, HEAD_DIM:] + upds[p][:, HEAD_DIM:]
        st_ref[2 * p], st_ref[2 * p + 1] = st0, st1
        sn_ref[p, :HEAD_DIM, :HEAD_DIM] = st0.T.astype(BF16)
        sn_ref[p, HEAD_DIM:, HEAD_DIM:] = st1.T.astype(BF16)
    return carry


def _conv_block(r, carry, *, ush_ref, cw_ref, cb_ref, y_s, rows_per):
    r0 = r * rows_per
    nsub = rows_per // SUB
    lead = TAIL - CONV_K + 1
    for lo in range(0, D_MODEL, CONV_LANES):
        lanes = slice(lo, lo + CONV_LANES)
        acc = [cb_ref[:, lanes]] * nsub
        for shift in range(SUB):
            taps = [j for j in range(CONV_K) if (lead + j) % SUB == shift]
            w = {j: cw_ref[j, :, lanes] for j in taps}
            first = min((lead + j) // SUB for j in taps)
            last = max((lead + j) // SUB for j in taps) + nsub - 1
            for m in range(first, last + 1):
                blk = ush_ref[shift, _aligned(r0 + (m + 1) * SUB, SUB), lanes]
                for j in taps:
                    b = m - (lead + j) // SUB
                    if 0 <= b < nsub:
                        acc[b] = acc[b] + blk * w[j]
        for b in range(nsub):
            y_s[_aligned(r0 + b * SUB, SUB), lanes] = acc[b]
    return carry


def _mixer_body(x_ref, mixn_ref, win_ref, lbl_ref, hn_ref, wo_ref, cw_ref, cb_ref, lng_ref,
                lnb_ref, wpw_ref, bpw_ref, wout_ref, o_ref,
                st_ref, sn_ref, tail_ref, ush_ref, gt_s, h_s, v_s, q_s, k_s, lf_s, o_s, y_s, *var_refs, layer):
    tm = x_ref.shape[1]
    names = LEVELS + ("c",)
    qv = dict(zip(names, var_refs[:len(names)]))
    kv = dict(zip(names, var_refs[len(names):]))

    @pl.when(pl.program_id(1) == 0)
    def _():
        st_ref[...] = jnp.zeros_like(st_ref)
        sn_ref[...] = jnp.zeros_like(sn_ref)
        tail_ref[...] = jnp.zeros_like(tail_ref)

    x = x_ref[0]
    h_s[...] = _rms(x, mixn_ref[...]).astype(BF16)

    def proj(i):
        return jnp.dot(h_s[...], win_ref[:, i * D_MODEL:(i + 1) * D_MODEL],
                       preferred_element_type=F32)

    lbl = lbl_ref[...]
    e = jnp.exp(lbl - jnp.max(lbl, axis=0, keepdims=True))
    lb = jnp.sum(e[0:layer + 1], axis=0, keepdims=True) / jnp.sum(e, axis=0, keepdims=True)

    u = proj(4) * _sigmoid(proj(5))
    ext = jnp.concatenate([tail_ref[...], u], axis=0)
    n_ext = TAIL + tm
    tiles = [ext[i:i + SUB] for i in range(0, n_ext, SUB)]
    rid = lax.broadcasted_iota(jnp.int32, (SUB, D_MODEL), 0)
    ush_ref[0, SUB:SUB + n_ext, :] = ext
    rot = tiles
    for r in range(SUB - 1, 0, -1):
        rot = [pltpu.roll(t, 1, 0) for t in rot]
        ush_ref[r, SUB:SUB + n_ext, :] = jnp.concatenate(
            [jnp.where(rid < SUB - r, rot[i], rot[(i + 1) % len(rot)]) for i in range(len(rot))],
            axis=0)
    tail_ref[...] = u[tm - TAIL:tm]

    f = lb + (1.0 - lb) * _sigmoid(proj(1))
    f = jnp.clip(f, 1e-6, 1.0)
    lf_s[...] = jnp.log(f)
    k_s[...] = 1.0 - f
    q_s[...] = proj(0)
    v_s[...] = proj(2).astype(BF16)

    n_chunks = tm // CHUNK
    for c in range(n_chunks):
        if c == 0:
            _hgrn_prep(0, 0, q_s=q_s, k_s=k_s, lf_s=lf_s, qv=qv, kv=kv, gt_s=gt_s)
        _hgrn_mm(c, 0, qv=qv, kv=kv, v_s=v_s, o_s=o_s, st_ref=st_ref, sn_ref=sn_ref, gt_s=gt_s)
        if c + 1 < n_chunks:
            _hgrn_prep(c + 1, 0, q_s=q_s, k_s=k_s, lf_s=lf_s, qv=qv, kv=kv, gt_s=gt_s)
    lax.fori_loop(0, tm // CONV_ROWS,
                  functools.partial(_conv_block, ush_ref=ush_ref, cw_ref=cw_ref, cb_ref=cb_ref,
                                    y_s=y_s, rows_per=CONV_ROWS), 0)

    o = o_s[...]
    hn = hn_ref[...]
    o = jnp.concatenate(
        [_rms(o[:, i * HEAD_DIM:(i + 1) * HEAD_DIM], hn) for i in range(HEADS)], axis=1)
    g_out = proj(3)
    y_a = jnp.dot((o * (g_out * _sigmoid(g_out))).astype(BF16), wo_ref[...],
                  preferred_element_type=F32)

    u = y_s[...]
    mu = jnp.mean(u, axis=-1, keepdims=True)
    uc = u - mu
    var = jnp.mean(uc * uc, axis=-1, keepdims=True)
    u = uc * lax.rsqrt(var + EPS) * lng_ref[...] + lnb_ref[...]
    u = u * _sigmoid(u)
    y_b = jnp.dot(u.astype(BF16), wpw_ref[...], preferred_element_type=F32) + bpw_ref[...]

    merged = _sigmoid(proj(6)) * y_a + _sigmoid(proj(7)) * y_b
    o_ref[0] = x + jnp.dot(merged.astype(BF16), wout_ref[...], preferred_element_type=F32)


def _mixer(x, mixn, win, lbl, hn, wo, cw, cb, lng, lnb, wpw, bpw, wout, *, layer):
    b, l, d = x.shape
    tm = MIX_TM
    act = lambda: pltpu.VMEM((tm, d), F32)
    act16 = lambda: pltpu.VMEM((tm, d), BF16)
    n_var = 2 * (len(LEVELS) + 1)
    return pl.pallas_call(
        functools.partial(_mixer_body, layer=layer),
        grid=(b, l // tm),
        in_specs=[
            pl.BlockSpec((1, tm, d), lambda i, j: (i, j, 0)),
            _resident(mixn.shape), _resident(win.shape), _resident(lbl.shape),
            _resident(hn.shape), _resident(wo.shape), _resident(cw.shape), _resident(cb.shape),
            _resident(lng.shape), _resident(lnb.shape), _resident(wpw.shape),
            _resident(bpw.shape), _resident(wout.shape),
        ],
        out_specs=pl.BlockSpec((1, tm, d), lambda i, j: (i, j, 0)),
        out_shape=jax.ShapeDtypeStruct((b, l, d), F32),
        scratch_shapes=[
            pltpu.VMEM((HEADS, HEAD_DIM, HEAD_DIM), F32),
            pltpu.VMEM((HEADS // 2, PAIR, PAIR), BF16),
            pltpu.VMEM((TAIL, d), F32),
            pltpu.VMEM((SUB, SUB + TAIL + tm, d), F32),
            pltpu.VMEM((tm // CHUNK * SUB, d), F32),
            act16(),
            act16(),
            act(), act(), act(), act(), act(),
        ] + [act16() for _ in range(n_var)],
        compiler_params=pltpu.CompilerParams(
            dimension_semantics=("arbitrary", "arbitrary"), vmem_limit_bytes=VMEM_LIMIT),
        name="mixer",
    )(x, mixn, win, lbl, hn, wo, cw, cb, lng, lnb, wpw, bpw, wout)


def kernel(x, ffn1_norm, ffn1_w_gate, ffn1_w_up, ffn1_w_down, mix_norm, w_in, hgrn_lb_logits, hgrn_head_norm, hgrn_w_o, conv_w, conv_b, conv_ln_g, conv_ln_b, conv_w_pw, conv_b_pw, w_out, ffn2_norm, ffn2_w_gate, ffn2_w_up, ffn2_w_down, final_norm):
    b, l, d = x.shape
    depth = ffn1_norm.shape[0]
    row = lambda a: a.reshape(1, -1)
    w16 = lambda a: a.astype(BF16)
    sub8 = lambda a: jnp.broadcast_to(a[..., None, :], a.shape[:-1] + (SUB, a.shape[-1]))
    fin = row(final_norm)
    for i in range(depth):
        x = _ffn(x.reshape(b * l, d), row(ffn1_norm[i]), ffn1_w_gate, ffn1_w_up, ffn1_w_down, fin,
                 layer=i, final=False).reshape(b, l, d)
        x = _mixer(x, row(mix_norm[i]), w16(w_in[i]), hgrn_lb_logits, row(hgrn_head_norm[i]),
                   w16(hgrn_w_o[i]), sub8(conv_w[i]), sub8(conv_b[i]), row(conv_ln_g[i]),
                   row(conv_ln_b[i]), w16(conv_w_pw[i]), row(conv_b_pw[i]), w16(w_out[i]), layer=i)
        last = i == depth - 1
        x = _ffn(x.reshape(b * l, d), row(ffn2_norm[i]), ffn2_w_gate, ffn2_w_up, ffn2_w_down, fin,
                 layer=i, final=last).reshape(b, l, d)
    return x
```

```python
import functools

import jax
import jax.numpy as jnp
from jax import lax
from jax.experimental import pallas as pl
from jax.experimental.pallas import tpu as pltpu

D_MODEL = 1024
D_FF = 2816
HEADS = 8
HEAD_DIM = 128
PAIR = 2 * HEAD_DIM
CONV_K = 31
FFN_RES = 0.5
EPS = 1e-6
N_SPLITS = 8

CHUNK = 64
SUB = 8
NBLK = CHUNK // SUB
TAIL = 32
MID = 3
CONV_ROWS = 128
CONV_LANES = 256
LEVELS = ("d", "8", "16", "32")

FFN_TM = 512
FFN_COLS = 1536
MIX_TM = 256
VMEM_LIMIT = 60 * 1024 * 1024

F32 = jnp.float32
BF16 = jnp.bfloat16
NT = (((1,), (1,)), ((), ()))
TN = (((0,), (0,)), ((), ()))


def _rms(x, g):
    ms = jnp.mean(x * x, axis=-1, keepdims=True)
    return x * lax.rsqrt(ms + EPS) * g


def _sigmoid(x):
    return 0.5 * jnp.tanh(0.5 * x) + 0.5


def _aligned(start, n):
    return pl.ds(start if isinstance(start, int) else pl.multiple_of(start, n), n)


def _block_rows(i, n):
    return _aligned(i * n, n)


def _resident(shape):
    nd = len(shape)
    return pl.BlockSpec(shape, lambda *_: (0,) * nd, pipeline_mode=pl.Buffered(1))


def _ffn_body(x_ref, nrm_ref, wg_ref, wu_ref, wd_ref, fin_ref, o_ref, *, final):
    x = x_ref[...]
    h = _rms(x, nrm_ref[...])
    y = x
    f = wg_ref.shape[1]
    for lo in range(0, f, FFN_COLS):
        cols = slice(lo, min(lo + FFN_COLS, f))
        g = jnp.dot(h, wg_ref[:, cols], preferred_element_type=F32)
        u = jnp.dot(h, wu_ref[:, cols], preferred_element_type=F32)
        y = y + FFN_RES * jnp.dot(g * _sigmoid(g) * u, wd_ref[cols, :],
                                  preferred_element_type=F32)
    if final:
        y = _rms(y, fin_ref[...])
    o_ref[...] = y


def _ffn(x2d, nrm, wg, wu, wd, fin, *, layer, final):
    t, d = x2d.shape
    f = wg.shape[2]
    tm = FFN_TM
    weight = lambda r, c: pl.BlockSpec((None, r, c), lambda i: (layer, 0, 0),
                                       pipeline_mode=pl.Buffered(1))
    return pl.pallas_call(
        functools.partial(_ffn_body, final=final),
        grid=(t // tm,),
        in_specs=[
            pl.BlockSpec((tm, d), lambda i: (i, 0)),
            _resident((1, d)),
            weight(d, f),
            weight(d, f),
            weight(f, d),
            _resident((1, d)),
        ],
        out_specs=pl.BlockSpec((tm, d), lambda i: (i, 0)),
        out_shape=jax.ShapeDtypeStruct((t, d), F32),
        compiler_params=pltpu.CompilerParams(
            dimension_semantics=("arbitrary",), vmem_limit_bytes=VMEM_LIMIT),
        name="ffn_final" if final else "ffn",
    )(x2d, nrm, wg, wu, wd, fin)


def _hgrn_prep(c, carry, *, q_s, k_s, lf_s, qv, kv, gt_s):
    rows = _block_rows(c, CHUNK)
    lf, q, k = lf_s[rows, :], q_s[rows, :], k_s[rows, :]
    rid = lax.broadcasted_iota(jnp.int32, (SUB, D_MODEL), 0)

    qe, kf, qd, kd, g = [], [], [], [], []
    for i in range(NBLK):
        sl = slice(i * SUB, (i + 1) * SUB)
        a = lf[sl]
        for s in (1, 2, 4):
            a = a + jnp.where(rid >= s, pltpu.roll(a, s, 0), 0.0)
        tot = jnp.broadcast_to(a[SUB - 1:SUB], (SUB, D_MODEL))
        dm = a - jnp.broadcast_to(a[MID:MID + 1], (SUB, D_MODEL))
        qe.append(q[sl] * jnp.exp(a))
        kf.append(k[sl] * jnp.exp(tot - a))
        qd.append(q[sl] * jnp.exp(dm))
        kd.append(k[sl] * jnp.exp(-dm))
        g.append(jnp.exp(tot))

    def running(blocks):
        out, acc = [None], None
        for blk in blocks:
            acc = blk if acc is None else acc * blk
            out.append(acc)
        return out

    def scaled(base, factors):
        return [b if f is None else b * f for b, f in zip(base, factors)]

    def store(ref, blocks):
        ref[rows, :] = jnp.concatenate(blocks, axis=0).astype(BF16)

    def q_factors(nb):
        out = []
        for lo in range(0, NBLK, nb):
            out += running(g[lo:lo + nb])[:nb]
        return out

    def k_factors(nb):
        out = []
        for lo in range(0, NBLK, nb):
            out += running(g[lo:lo + nb][::-1])[:nb][::-1]
        return out

    store(qv["d"], qd)
    store(kv["d"], kd)
    store(qv["8"], qe)
    store(kv["8"], kf)
    for name, nb in (("16", 2), ("32", 4), ("c", NBLK)):
        store(qv[name], scaled(qe, q_factors(nb)))
        store(kv[name], scaled(kf, k_factors(nb)))
    gt_s[_block_rows(c, SUB), :] = running(g)[NBLK]
    return carry


def _hgrn_mm(c, carry, *, qv, kv, v_s, o_s, st_ref, sn_ref, gt_s):
    rows = _block_rows(c, CHUNK)
    gt = gt_s[_block_rows(c, SUB), :]
    v = [v_s[rows, p * PAIR:(p + 1) * PAIR] for p in range(HEADS // 2)]

    ti = lax.broadcasted_iota(jnp.int32, (CHUNK, 2 * CHUNK), 0)
    si = lax.broadcasted_iota(jnp.int32, (CHUNK, 2 * CHUNK), 1) % CHUNK
    bt, bs = ti // SUB, si // SUB
    masks = {
        "d": (bt == bs) & (si <= ti),
        "8": (bt == bs + 1) & (bt % 2 == 1),
        "16": (bt // 4 == bs // 4) & ((bt // 2) % 2 == 1) & ((bs // 2) % 2 == 0),
        "32": (bt // 4 == 1) & (bs // 4 == 0),
    }

    def bdiag(x):
        z = jnp.zeros((x.shape[0], HEAD_DIM), x.dtype)
        return jnp.concatenate(
            [jnp.concatenate([x[:, :HEAD_DIM], z], axis=1),
             jnp.concatenate([z, x[:, HEAD_DIM:]], axis=1)], axis=0)

    pairs = [slice(p * PAIR, (p + 1) * PAIR) for p in range(HEADS // 2)]
    scores = [{name: lax.dot_general(qv[name][rows, ps], bdiag(kv[name][rows, ps]), NT,
                                     preferred_element_type=F32) for name in LEVELS}
              for ps in pairs]
    states = [(st_ref[2 * p], st_ref[2 * p + 1]) for p in range(len(pairs))]
    inter = [jnp.dot(qv["c"][rows, ps], sn_ref[p], preferred_element_type=F32)
             for p, ps in enumerate(pairs)]
    upds = [lax.dot_general(jnp.concatenate([v[p][:, :HEAD_DIM], v[p][:, HEAD_DIM:]], axis=0),
                            bdiag(kv["c"][rows, ps]), TN, preferred_element_type=F32)
            for p, ps in enumerate(pairs)]
    for p, ps in enumerate(pairs):
        pm = jnp.zeros((CHUNK, 2 * CHUNK), F32)
        for name in LEVELS:
            pm = jnp.where(masks[name], scores[p][name], pm)
        o_s[rows, ps] = inter[p] + jnp.dot(pm.astype(BF16), bdiag(v[p]),
                                           preferred_element_type=F32)
        g_p = jnp.tile(gt[:, ps], (HEAD_DIM // SUB, 1))
        st0, st1 = states[p]
        st0 = st0 * g_p[:, :HEAD_DIM] + upds[p][:, :HEAD_DIM]
        st1 = st1 * g_p[:, HEAD_DIM:] + upds[p][:, HEAD_DIM:]
        st_ref[2 * p], st_ref[2 * p + 1] = st0, st1
        sn_ref[p, :HEAD_DIM, :HEAD_DIM] = st0.T.astype(BF16)
        sn_ref[p, HEAD_DIM:, HEAD_DIM:] = st1.T.astype(BF16)
    return carry


def _conv_block(r, carry, *, ush_ref, cw_ref, cb_ref, y_s, rows_per):
    r0 = r * rows_per
    nsub = rows_per // SUB
    lead = TAIL - CONV_K + 1
    for lo in range(0, D_MODEL, CONV_LANES):
        lanes = slice(lo, lo + CONV_LANES)
        acc = [cb_ref[:, lanes]] * nsub
        for shift in range(SUB):
            taps = [j for j in range(CONV_K) if (lead + j) % SUB == shift]
            w = {j: cw_ref[j, :, lanes] for j in taps}
            first = min((lead + j) // SUB for j in taps)
            last = max((lead + j) // SUB for j in taps) + nsub - 1
            for m in range(first, last + 1):
                blk = ush_ref[shift, _aligned(r0 + (m + 1) * SUB, SUB), lanes]
                for j in taps:
                    b = m - (lead + j) // SUB
                    if 0 <= b < nsub:
                        acc[b] = acc[b] + blk * w[j]
        for b in range(nsub):
            y_s[_aligned(r0 + b * SUB, SUB), lanes] = acc[b]
    return carry


def _mixer_body(x_ref, mixn_ref, win_ref, lbl_ref, hn_ref, wo_ref, cw_ref, cb_ref, lng_ref,
                lnb_ref, wpw_ref, bpw_ref, wout_ref, o_ref,
                st_ref, sn_ref, tail_ref, ush_ref, gt_s, h_s, v_s, q_s, k_s, lf_s, o_s, y_s, *var_refs, layer):
    tm = x_ref.shape[1]
    names = LEVELS + ("c",)
    qv = dict(zip(names, var_refs[:len(names)]))
    kv = dict(zip(names, var_refs[len(names):]))

    @pl.when(pl.program_id(1) == 0)
    def _():
        st_ref[...] = jnp.zeros_like(st_ref)
        sn_ref[...] = jnp.zeros_like(sn_ref)
        tail_ref[...] = jnp.zeros_like(tail_ref)

    x = x_ref[0]
    h_s[...] = _rms(x, mixn_ref[...]).astype(BF16)

    def proj(i):
        return jnp.dot(h_s[...], win_ref[:, i * D_MODEL:(i + 1) * D_MODEL],
                       preferred_element_type=F32)

    lbl = lbl_ref[...]
    e = jnp.exp(lbl - jnp.max(lbl, axis=0, keepdims=True))
    lb = jnp.sum(e[0:layer + 1], axis=0, keepdims=True) / jnp.sum(e, axis=0, keepdims=True)

    u = proj(4) * _sigmoid(proj(5))
    ext = jnp.concatenate([tail_ref[...], u], axis=0)
    n_ext = TAIL + tm
    tiles = [ext[i:i + SUB] for i in range(0, n_ext, SUB)]
    rid = lax.broadcasted_iota(jnp.int32, (SUB, D_MODEL), 0)
    ush_ref[0, SUB:SUB + n_ext, :] = ext
    rot = tiles
    for r in range(SUB - 1, 0, -1):
        rot = [pltpu.roll(t, 1, 0) for t in rot]
        ush_ref[r, SUB:SUB + n_ext, :] = jnp.concatenate(
            [jnp.where(rid < SUB - r, rot[i], rot[(i + 1) % len(rot)]) for i in range(len(rot))],
            axis=0)
    tail_ref[...] = u[tm - TAIL:tm]

    f = lb + (1.0 - lb) * _sigmoid(proj(1))
    f = jnp.clip(f, 1e-6, 1.0)
    lf_s[...] = jnp.log(f)
    k_s[...] = 1.0 - f
    q_s[...] = proj(0)
    v_s[...] = proj(2).astype(BF16)

    n_chunks = tm // CHUNK
    for c in range(n_chunks):
        if c == 0:
            _hgrn_prep(0, 0, q_s=q_s, k_s=k_s, lf_s=lf_s, qv=qv, kv=kv, gt_s=gt_s)
        _hgrn_mm(c, 0, qv=qv, kv=kv, v_s=v_s, o_s=o_s, st_ref=st_ref, sn_ref=sn_ref, gt_s=gt_s)
        if c + 1 < n_chunks:
            _hgrn_prep(c + 1, 0, q_s=q_s, k_s=k_s, lf_s=lf_s, qv=qv, kv=kv, gt_s=gt_s)
    lax.fori_loop(0, tm // CONV_ROWS,
                  functools.partial(_conv_block, ush_ref=ush_ref, cw_ref=cw_ref, cb_ref=cb_ref,
                                    y_s=y_s, rows_per=CONV_ROWS), 0)

    o = o_s[...]
    hn = hn_ref[...]
    o = jnp.concatenate(
        [_rms(o[:, i * HEAD_DIM:(i + 1) * HEAD_DIM], hn) for i in range(HEADS)], axis=1)
    g_out = proj(3)
    y_a = jnp.dot(o * (g_out * _sigmoid(g_out)), wo_ref[...],
                  preferred_element_type=F32)

    u = y_s[...]
    mu = jnp.mean(u, axis=-1, keepdims=True)
    uc = u - mu
    var = jnp.mean(uc * uc, axis=-1, keepdims=True)
    u = uc * lax.rsqrt(var + EPS) * lng_ref[...] + lnb_ref[...]
    u = u * _sigmoid(u)
    y_b = jnp.dot(u, wpw_ref[...], preferred_element_type=F32) + bpw_ref[...]

    merged = _sigmoid(proj(6)) * y_a + _sigmoid(proj(7)) * y_b
    o_ref[0] = x + jnp.dot(merged, wout_ref[...], preferred_element_type=F32)


def _mixer(x, mixn, win, lbl, hn, wo, cw, cb, lng, lnb, wpw, bpw, wout, *, layer):
    b, l, d = x.shape
    tm = MIX_TM
    act = lambda: pltpu.VMEM((tm, d), F32)
    act16 = lambda: pltpu.VMEM((tm, d), BF16)
    n_var = 2 * (len(LEVELS) + 1)
    square = pl.BlockSpec((None, d, d), lambda i, j: (layer, 0, 0), pipeline_mode=pl.Buffered(1))
    return pl.pallas_call(
        functools.partial(_mixer_body, layer=layer),
        grid=(b, l // tm),
        in_specs=[
            pl.BlockSpec((1, tm, d), lambda i, j: (i, j, 0)),
            _resident(mixn.shape), _resident(win.shape), _resident(lbl.shape),
            _resident(hn.shape), square, _resident(cw.shape), _resident(cb.shape),
            _resident(lng.shape), _resident(lnb.shape), square,
            _resident(bpw.shape), square,
        ],
        out_specs=pl.BlockSpec((1, tm, d), lambda i, j: (i, j, 0)),
        out_shape=jax.ShapeDtypeStruct((b, l, d), F32),
        scratch_shapes=[
            pltpu.VMEM((HEADS, HEAD_DIM, HEAD_DIM), F32),
            pltpu.VMEM((HEADS // 2, PAIR, PAIR), BF16),
            pltpu.VMEM((TAIL, d), F32),
            pltpu.VMEM((SUB, SUB + TAIL + tm, d), F32),
            pltpu.VMEM((tm // CHUNK * SUB, d), F32),
            act16(),
            act16(),
            act(), act(), act(), act(), act(),
        ] + [act16() for _ in range(n_var)],
        compiler_params=pltpu.CompilerParams(
            dimension_semantics=("arbitrary", "arbitrary"), vmem_limit_bytes=VMEM_LIMIT),
        name="mixer",
    )(x, mixn, win, lbl, hn, wo, cw, cb, lng, lnb, wpw, bpw, wout)


def kernel(x, ffn1_norm, ffn1_w_gate, ffn1_w_up, ffn1_w_down, mix_norm, w_in, hgrn_lb_logits, hgrn_head_norm, hgrn_w_o, conv_w, conv_b, conv_ln_g, conv_ln_b, conv_w_pw, conv_b_pw, w_out, ffn2_norm, ffn2_w_gate, ffn2_w_up, ffn2_w_down, final_norm):
    b, l, d = x.shape
    depth = ffn1_norm.shape[0]
    row = lambda a: a.reshape(1, -1)
    w16 = lambda a: a.astype(BF16)
    sub8 = lambda a: jnp.broadcast_to(a[..., None, :], a.shape[:-1] + (SUB, a.shape[-1]))
    fin = row(final_norm)
    for i in range(depth):
        x = _ffn(x.reshape(b * l, d), row(ffn1_norm[i]), ffn1_w_gate, ffn1_w_up, ffn1_w_down, fin,
                 layer=i, final=False).reshape(b, l, d)
        x = _mixer(x, row(mix_norm[i]), w16(w_in[i]), hgrn_lb_logits, row(hgrn_head_norm[i]),
                   hgrn_w_o, sub8(conv_w[i]), sub8(conv_b[i]), row(conv_ln_g[i]),
                   row(conv_ln_b[i]), conv_w_pw, row(conv_b_pw[i]), w_out, layer=i)
        last = i == depth - 1
        x = _ffn(x.reshape(b * l, d), row(ffn2_norm[i]), ffn2_w_gate, ffn2_w_up, ffn2_w_down, fin,
                 layer=i, final=last).reshape(b, l, d)
    return x
```

```python
import functools

import jax
import jax.numpy as jnp
from jax import lax
from jax.experimental import pallas as pl
from jax.experimental.pallas import tpu as pltpu

D_MODEL = 1024
D_FF = 2816
HEADS = 8
HEAD_DIM = 128
PAIR = 2 * HEAD_DIM
CONV_K = 31
FFN_RES = 0.5
EPS = 1e-6
N_SPLITS = 8

CHUNK = 64
SUB = 8
NBLK = CHUNK // SUB
TAIL = 32
MID = 3
CONV_ROWS = 128
CONV_LANES = 256
LEVELS = ("d", "8", "16", "32")

FFN_TM = 1024
FFN_ROWS = 256
FFN_COLS = 1536
MIX_TM = 256
VMEM_LIMIT = 60 * 1024 * 1024

F32 = jnp.float32
BF16 = jnp.bfloat16
NT = (((1,), (1,)), ((), ()))
TN = (((0,), (0,)), ((), ()))


def _rms(x, g):
    ms = jnp.mean(x * x, axis=-1, keepdims=True)
    return x * lax.rsqrt(ms + EPS) * g


def _sigmoid(x):
    return 0.5 * jnp.tanh(0.5 * x) + 0.5


def _aligned(start, n):
    return pl.ds(start if isinstance(start, int) else pl.multiple_of(start, n), n)


def _block_rows(i, n):
    return _aligned(i * n, n)


def _resident(shape):
    nd = len(shape)
    return pl.BlockSpec(shape, lambda *_: (0,) * nd, pipeline_mode=pl.Buffered(1))


def _ffn_body(x_ref, nrm_ref, wg_ref, wu_ref, wd_ref, fin_ref, o_ref, *, final):
    f = wg_ref.shape[1]
    for r0 in range(0, x_ref.shape[0], FFN_ROWS):
        rows = slice(r0, r0 + FFN_ROWS)
        x = x_ref[rows, :]
        h = _rms(x, nrm_ref[...])
        y = x
        for lo in range(0, f, FFN_COLS):
            cols = slice(lo, min(lo + FFN_COLS, f))
            g = jnp.dot(h, wg_ref[:, cols], preferred_element_type=F32)
            u = jnp.dot(h, wu_ref[:, cols], preferred_element_type=F32)
            y = y + FFN_RES * jnp.dot(g * _sigmoid(g) * u, wd_ref[cols, :],
                                      preferred_element_type=F32)
        if final:
            y = _rms(y, fin_ref[...])
        o_ref[rows, :] = y


def _ffn(x2d, nrm, wg, wu, wd, fin, *, layer, final):
    t, d = x2d.shape
    f = wg.shape[2]
    tm = FFN_TM
    weight = lambda r, c: pl.BlockSpec((None, r, c), lambda i: (layer, 0, 0),
                                       pipeline_mode=pl.Buffered(1))
    return pl.pallas_call(
        functools.partial(_ffn_body, final=final),
        grid=(t // tm,),
        in_specs=[
            pl.BlockSpec((tm, d), lambda i: (i, 0)),
            _resident((1, d)),
            weight(d, f),
            weight(d, f),
            weight(f, d),
            _resident((1, d)),
        ],
        out_specs=pl.BlockSpec((tm, d), lambda i: (i, 0)),
        out_shape=jax.ShapeDtypeStruct((t, d), F32),
        compiler_params=pltpu.CompilerParams(
            dimension_semantics=("arbitrary",), vmem_limit_bytes=VMEM_LIMIT),
        name="ffn_final" if final else "ffn",
    )(x2d, nrm, wg, wu, wd, fin)


def _hgrn_prep(c, carry, *, q_s, k_s, lf_s, qv, kv, gt_s):
    rows = _block_rows(c, CHUNK)
    lf, q, k = lf_s[rows, :], q_s[rows, :], k_s[rows, :]
    rid = lax.broadcasted_iota(jnp.int32, (SUB, D_MODEL), 0)

    qe, kf, qd, kd, g = [], [], [], [], []
    for i in range(NBLK):
        sl = slice(i * SUB, (i + 1) * SUB)
        a = lf[sl]
        for s in (1, 2, 4):
            a = a + jnp.where(rid >= s, pltpu.roll(a, s, 0), 0.0)
        tot = jnp.broadcast_to(a[SUB - 1:SUB], (SUB, D_MODEL))
        dm = a - jnp.broadcast_to(a[MID:MID + 1], (SUB, D_MODEL))
        qe.append(q[sl] * jnp.exp(a))
        kf.append(k[sl] * jnp.exp(tot - a))
        qd.append(q[sl] * jnp.exp(dm))
        kd.append(k[sl] * jnp.exp(-dm))
        g.append(jnp.exp(tot))

    def running(blocks):
        out, acc = [None], None
        for blk in blocks:
            acc = blk if acc is None else acc * blk
            out.append(acc)
        return out

    def scaled(base, factors):
        return [b if f is None else b * f for b, f in zip(base, factors)]

    def store(ref, blocks):
        ref[rows, :] = jnp.concatenate(blocks, axis=0).astype(BF16)

    def q_factors(nb):
        out = []
        for lo in range(0, NBLK, nb):
            out += running(g[lo:lo + nb])[:nb]
        return out

    def k_factors(nb):
        out = []
        for lo in range(0, NBLK, nb):
            out += running(g[lo:lo + nb][::-1])[:nb][::-1]
        return out

    store(qv["d"], qd)
    store(kv["d"], kd)
    store(qv["8"], qe)
    store(kv["8"], kf)
    for name, nb in (("16", 2), ("32", 4), ("c", NBLK)):
        store(qv[name], scaled(qe, q_factors(nb)))
        store(kv[name], scaled(kf, k_factors(nb)))
    gt_s[_block_rows(c, SUB), :] = running(g)[NBLK]
    return carry


def _hgrn_mm(c, carry, *, qv, kv, v_s, o_s, st_ref, sn_ref, gt_s):
    rows = _block_rows(c, CHUNK)
    gt = gt_s[_block_rows(c, SUB), :]
    v = [v_s[rows, p * PAIR:(p + 1) * PAIR] for p in range(HEADS // 2)]

    ti = lax.broadcasted_iota(jnp.int32, (CHUNK, 2 * CHUNK), 0)
    si = lax.broadcasted_iota(jnp.int32, (CHUNK, 2 * CHUNK), 1) % CHUNK
    bt, bs = ti // SUB, si // SUB
    masks = {
        "d": (bt == bs) & (si <= ti),
        "8": (bt == bs + 1) & (bt % 2 == 1),
        "16": (bt // 4 == bs // 4) & ((bt // 2) % 2 == 1) & ((bs // 2) % 2 == 0),
        "32": (bt // 4 == 1) & (bs // 4 == 0),
    }

    def bdiag(x):
        z = jnp.zeros((x.shape[0], HEAD_DIM), x.dtype)
        return jnp.concatenate(
            [jnp.concatenate([x[:, :HEAD_DIM], z], axis=1),
             jnp.concatenate([z, x[:, HEAD_DIM:]], axis=1)], axis=0)

    pairs = [slice(p * PAIR, (p + 1) * PAIR) for p in range(HEADS // 2)]
    scores = [{name: lax.dot_general(qv[name][rows, ps], bdiag(kv[name][rows, ps]), NT,
                                     preferred_element_type=F32) for name in LEVELS}
              for ps in pairs]
    states = [(st_ref[2 * p], st_ref[2 * p + 1]) for p in range(len(pairs))]
    inter = [jnp.dot(qv["c"][rows, ps], sn_ref[p], preferred_element_type=F32)
             for p, ps in enumerate(pairs)]
    upds = [lax.dot_general(jnp.concatenate([v[p][:, :HEAD_DIM], v[p][:, HEAD_DIM:]], axis=0),
                            bdiag(kv["c"][rows, ps]), TN, preferred_element_type=F32)
            for p, ps in enumerate(pairs)]
    for p, ps in enumerate(pairs):
        pm = jnp.zeros((CHUNK, 2 * CHUNK), F32)
        for name in LEVELS:
            pm = jnp.where(masks[name], scores[p][name], pm)
        o_s[rows, ps] = inter[p] + jnp.dot(pm.astype(BF16), bdiag(v[p]),
                                           preferred_element_type=F32)
        g_p = jnp.tile(gt[:, ps], (HEAD_DIM // SUB, 1))
        st0, st1 = states[p]
        st0 = st0 * g_p[:, :HEAD_DIM] + upds[p][:, :HEAD_DIM]
        st1 = st1 * g_p[:, HEAD_DIM:] + upds[p][:, HEAD_DIM:]
        st_ref[2 * p], st_ref[2 * p + 1] = st0, st1
        sn_ref[p, :HEAD_DIM, :HEAD_DIM] = st0.T.astype(BF16)
        sn_ref[p, HEAD_DIM:, HEAD_DIM:] = st1.T.astype(BF16)
    return carry


def _conv_block(r, carry, *, ush_ref, cw_ref, cb_ref, y_s, rows_per):
    r0 = r * rows_per
    nsub = rows_per // SUB
    lead = TAIL - CONV_K + 1
    for lo in range(0, D_MODEL, CONV_LANES):
        lanes = slice(lo, lo + CONV_LANES)
        acc = [cb_ref[:, lanes]] * nsub
        for shift in range(SUB):
            taps = [j for j in range(CONV_K) if (lead + j) % SUB == shift]
            w = {j: cw_ref[j, :, lanes] for j in taps}
            first = min((lead + j) // SUB for j in taps)
            last = max((lead + j) // SUB for j in taps) + nsub - 1
            for m in range(first, last + 1):
                blk = ush_ref[shift, _aligned(r0 + (m + 1) * SUB, SUB), lanes]
                for j in taps:
                    b = m - (lead + j) // SUB
                    if 0 <= b < nsub:
                        acc[b] = acc[b] + blk * w[j]
        for b in range(nsub):
            y_s[_aligned(r0 + b * SUB, SUB), lanes] = acc[b]
    return carry


def _mixer_body(x_ref, mixn_ref, win_ref, lbl_ref, hn_ref, wo_ref, cw_ref, cb_ref, lng_ref,
                lnb_ref, wpw_ref, bpw_ref, wout_ref, o_ref,
                st_ref, sn_ref, tail_ref, ush_ref, gt_s, h_s, v_s, q_s, k_s, lf_s, o_s, y_s, *var_refs, layer):
    tm = x_ref.shape[1]
    names = LEVELS + ("c",)
    qv = dict(zip(names, var_refs[:len(names)]))
    kv = dict(zip(names, var_refs[len(names):]))

    @pl.when(pl.program_id(1) == 0)
    def _():
        st_ref[...] = jnp.zeros_like(st_ref)
        sn_ref[...] = jnp.zeros_like(sn_ref)
        tail_ref[...] = jnp.zeros_like(tail_ref)

    x = x_ref[0]
    h_s[...] = _rms(x, mixn_ref[...]).astype(BF16)

    def proj(i):
        return jnp.dot(h_s[...], win_ref[:, i * D_MODEL:(i + 1) * D_MODEL],
                       preferred_element_type=F32)

    lbl = lbl_ref[...]
    e = jnp.exp(lbl - jnp.max(lbl, axis=0, keepdims=True))
    lb = jnp.sum(e[0:layer + 1], axis=0, keepdims=True) / jnp.sum(e, axis=0, keepdims=True)

    u = proj(4) * _sigmoid(proj(5))
    ext = jnp.concatenate([tail_ref[...], u], axis=0)
    n_ext = TAIL + tm
    tiles = [ext[i:i + SUB] for i in range(0, n_ext, SUB)]
    rid = lax.broadcasted_iota(jnp.int32, (SUB, D_MODEL), 0)
    ush_ref[0, SUB:SUB + n_ext, :] = ext
    rot = tiles
    for r in range(SUB - 1, 0, -1):
        rot = [pltpu.roll(t, 1, 0) for t in rot]
        ush_ref[r, SUB:SUB + n_ext, :] = jnp.concatenate(
            [jnp.where(rid < SUB - r, rot[i], rot[(i + 1) % len(rot)]) for i in range(len(rot))],
            axis=0)
    tail_ref[...] = u[tm - TAIL:tm]

    f = lb + (1.0 - lb) * _sigmoid(proj(1))
    f = jnp.clip(f, 1e-6, 1.0)
    lf_s[...] = jnp.log(f)
    k_s[...] = 1.0 - f
    q_s[...] = proj(0)
    v_s[...] = proj(2).astype(BF16)

    n_chunks = tm // CHUNK
    for c in range(n_chunks):
        if c == 0:
            _hgrn_prep(0, 0, q_s=q_s, k_s=k_s, lf_s=lf_s, qv=qv, kv=kv, gt_s=gt_s)
        _hgrn_mm(c, 0, qv=qv, kv=kv, v_s=v_s, o_s=o_s, st_ref=st_ref, sn_ref=sn_ref, gt_s=gt_s)
        if c + 1 < n_chunks:
            _hgrn_prep(c + 1, 0, q_s=q_s, k_s=k_s, lf_s=lf_s, qv=qv, kv=kv, gt_s=gt_s)
    lax.fori_loop(0, tm // CONV_ROWS,
                  functools.partial(_conv_block, ush_ref=ush_ref, cw_ref=cw_ref, cb_ref=cb_ref,
                                    y_s=y_s, rows_per=CONV_ROWS), 0)

    o = o_s[...]
    hn = hn_ref[...]
    o = jnp.concatenate(
        [_rms(o[:, i * HEAD_DIM:(i + 1) * HEAD_DIM], hn) for i in range(HEADS)], axis=1)
    g_out = proj(3)
    y_a = jnp.dot((o * (g_out * _sigmoid(g_out))).astype(BF16), wo_ref[...],
                  preferred_element_type=F32)

    u = y_s[...]
    mu = jnp.mean(u, axis=-1, keepdims=True)
    uc = u - mu
    var = jnp.mean(uc * uc, axis=-1, keepdims=True)
    u = uc * lax.rsqrt(var + EPS) * lng_ref[...] + lnb_ref[...]
    u = u * _sigmoid(u)
    y_b = jnp.dot(u.astype(BF16), wpw_ref[...], preferred_element_type=F32) + bpw_ref[...]

    merged = _sigmoid(proj(6)) * y_a + _sigmoid(proj(7)) * y_b
    o_ref[0] = x + jnp.dot(merged.astype(BF16), wout_ref[...], preferred_element_type=F32)


def _mixer(x, mixn, win, lbl, hn, wo, cw, cb, lng, lnb, wpw, bpw, wout, *, layer):
    b, l, d = x.shape
    tm = MIX_TM
    act = lambda: pltpu.VMEM((tm, d), F32)
    act16 = lambda: pltpu.VMEM((tm, d), BF16)
    n_var = 2 * (len(LEVELS) + 1)
    return pl.pallas_call(
        functools.partial(_mixer_body, layer=layer),
        grid=(b, l // tm),
        in_specs=[
            pl.BlockSpec((1, tm, d), lambda i, j: (i, j, 0)),
            _resident(mixn.shape), _resident(win.shape), _resident(lbl.shape),
            _resident(hn.shape), _resident(wo.shape), _resident(cw.shape), _resident(cb.shape),
            _resident(lng.shape), _resident(lnb.shape), _resident(wpw.shape),
            _resident(bpw.shape), _resident(wout.shape),
        ],
        out_specs=pl.BlockSpec((1, tm, d), lambda i, j: (i, j, 0)),
        out_shape=jax.ShapeDtypeStruct((b, l, d), F32),
        scratch_shapes=[
            pltpu.VMEM((HEADS, HEAD_DIM, HEAD_DIM), F32),
            pltpu.VMEM((HEADS // 2, PAIR, PAIR), BF16),
            pltpu.VMEM((TAIL, d), F32),
            pltpu.VMEM((SUB, SUB + TAIL + tm, d), F32),
            pltpu.VMEM((tm // CHUNK * SUB, d), F32),
            act16(),
            act16(),
            act(), act(), act(), act(), act(),
        ] + [act16() for _ in range(n_var)],
        compiler_params=pltpu.CompilerParams(
            dimension_semantics=("arbitrary", "arbitrary"), vmem_limit_bytes=VMEM_LIMIT),
        name="mixer",
    )(x, mixn, win, lbl, hn, wo, cw, cb, lng, lnb, wpw, bpw, wout)


def kernel(x, ffn1_norm, ffn1_w_gate, ffn1_w_up, ffn1_w_down, mix_norm, w_in, hgrn_lb_logits, hgrn_head_norm, hgrn_w_o, conv_w, conv_b, conv_ln_g, conv_ln_b, conv_w_pw, conv_b_pw, w_out, ffn2_norm, ffn2_w_gate, ffn2_w_up, ffn2_w_down, final_norm):
    b, l, d = x.shape
    depth = ffn1_norm.shape[0]
    row = lambda a: a.reshape(1, -1)
    w16 = lambda a: a.astype(BF16)
    sub8 = lambda a: jnp.broadcast_to(a[..., None, :], a.shape[:-1] + (SUB, a.shape[-1]))
    fin = row(final_norm)
    for i in range(depth):
        x = _ffn(x.reshape(b * l, d), row(ffn1_norm[i]), ffn1_w_gate, ffn1_w_up, ffn1_w_down, fin,
                 layer=i, final=False).reshape(b, l, d)
        x = _mixer(x, row(mix_norm[i]), w16(w_in[i]), hgrn_lb_logits, row(hgrn_head_norm[i]),
                   w16(hgrn_w_o[i]), sub8(conv_w[i]), sub8(conv_b[i]), row(conv_ln_g[i]),
                   row(conv_ln_b[i]), w16(conv_w_pw[i]), row(conv_b_pw[i]), w16(w_out[i]), layer=i)
        last = i == depth - 1
        x = _ffn(x.reshape(b * l, d), row(ffn2_norm[i]), ffn2_w_gate, ffn2_w_up, ffn2_w_down, fin,
                 layer=i, final=last).reshape(b, l, d)
    return x
```

```python
import functools

import jax
import jax.numpy as jnp
from jax import lax
from jax.experimental import pallas as pl
from jax.experimental.pallas import tpu as pltpu

D_MODEL = 1024
D_FF = 2816
HEADS = 8
HEAD_DIM = 128
PAIR = 2 * HEAD_DIM
CONV_K = 31
FFN_RES = 0.5
EPS = 1e-6
N_SPLITS = 8

CHUNK = 64
SUB = 8
NBLK = CHUNK // SUB
TAIL = 32
MID = 3
CONV_ROWS = 128
CONV_LANES = 256
LEVELS = ("d", "8", "16", "32")

FFN_TM = 1024
FFN_ROWS = 256
FFN_COLS = 1536
MIX_TM = 256
VMEM_LIMIT = 60 * 1024 * 1024

F32 = jnp.float32
BF16 = jnp.bfloat16
NT = (((1,), (1,)), ((), ()))
TN = (((0,), (0,)), ((), ()))


def _rms(x, g):
    ms = jnp.mean(x * x, axis=-1, keepdims=True)
    return x * lax.rsqrt(ms + EPS) * g


def _sigmoid(x):
    return 0.5 * jnp.tanh(0.5 * x) + 0.5


def _aligned(start, n):
    return pl.ds(start if isinstance(start, int) else pl.multiple_of(start, n), n)


def _block_rows(i, n):
    return _aligned(i * n, n)


def _resident(shape):
    nd = len(shape)
    return pl.BlockSpec(shape, lambda *_: (0,) * nd, pipeline_mode=pl.Buffered(1))


def _ffn_body(x_ref, nrm_ref, wg_ref, wu_ref, wd_ref, fin_ref, o_ref, *, final):
    f = wg_ref.shape[1]
    for r0 in range(0, x_ref.shape[0], FFN_ROWS):
        rows = slice(r0, r0 + FFN_ROWS)
        x = x_ref[rows, :]
        h = _rms(x, nrm_ref[...])
        y = x
        for lo in range(0, f, FFN_COLS):
            cols = slice(lo, min(lo + FFN_COLS, f))
            g = jnp.dot(h, wg_ref[:, cols], preferred_element_type=F32)
            u = jnp.dot(h, wu_ref[:, cols], preferred_element_type=F32)
            y = y + FFN_RES * jnp.dot(g * _sigmoid(g) * u, wd_ref[cols, :],
                                      preferred_element_type=F32)
        if final:
            y = _rms(y, fin_ref[...])
        o_ref[rows, :] = y


def _ffn(x2d, nrm, wg, wu, wd, fin, *, layer, final):
    t, d = x2d.shape
    f = wg.shape[2]
    tm = FFN_TM
    weight = lambda r, c: pl.BlockSpec((None, r, c), lambda i: (layer, 0, 0),
                                       pipeline_mode=pl.Buffered(1))
    return pl.pallas_call(
        functools.partial(_ffn_body, final=final),
        grid=(t // tm,),
        in_specs=[
            pl.BlockSpec((tm, d), lambda i: (i, 0)),
            _resident((1, d)),
            weight(d, f),
            weight(d, f),
            weight(f, d),
            _resident((1, d)),
        ],
        out_specs=pl.BlockSpec((tm, d), lambda i: (i, 0)),
        out_shape=jax.ShapeDtypeStruct((t, d), F32),
        compiler_params=pltpu.CompilerParams(
            dimension_semantics=("arbitrary",), vmem_limit_bytes=VMEM_LIMIT),
        name="ffn_final" if final else "ffn",
    )(x2d, nrm, wg, wu, wd, fin)


def _hgrn_prep(c, carry, *, q_s, k_s, lf_s, qv, kv, gt_s):
    rows = _block_rows(c, CHUNK)
    lf, q, k = lf_s[rows, :], q_s[rows, :], k_s[rows, :]
    rid = lax.broadcasted_iota(jnp.int32, (SUB, D_MODEL), 0)

    qe, kf, qd, kd, g = [], [], [], [], []
    for i in range(NBLK):
        sl = slice(i * SUB, (i + 1) * SUB)
        a = lf[sl]
        for s in (1, 2, 4):
            a = a + jnp.where(rid >= s, pltpu.roll(a, s, 0), 0.0)
        tot = jnp.broadcast_to(a[SUB - 1:SUB], (SUB, D_MODEL))
        dm = a - jnp.broadcast_to(a[MID:MID + 1], (SUB, D_MODEL))
        qe.append(q[sl] * jnp.exp(a))
        kf.append(k[sl] * jnp.exp(tot - a))
        qd.append(q[sl] * jnp.exp(dm))
        kd.append(k[sl] * jnp.exp(-dm))
        g.append(jnp.exp(tot))

    def running(blocks):
        out, acc = [None], None
        for blk in blocks:
            acc = blk if acc is None else acc * blk
            out.append(acc)
        return out

    def scaled(base, factors):
        return [b if f is None else b * f for b, f in zip(base, factors)]

    def store(ref, blocks):
        ref[rows, :] = jnp.concatenate(blocks, axis=0).astype(BF16)

    def q_factors(nb):
        out = []
        for lo in range(0, NBLK, nb):
            out += running(g[lo:lo + nb])[:nb]
        return out

    def k_factors(nb):
        out = []
        for lo in range(0, NBLK, nb):
            out += running(g[lo:lo + nb][::-1])[:nb][::-1]
        return out

    store(qv["d"], qd)
    store(kv["d"], kd)
    store(qv["8"], qe)
    store(kv["8"], kf)
    for name, nb in (("16", 2), ("32", 4), ("c", NBLK)):
        store(qv[name], scaled(qe, q_factors(nb)))
        store(kv[name], scaled(kf, k_factors(nb)))
    gt_s[_block_rows(c, SUB), :] = running(g)[NBLK]
    return carry


def _hgrn_mm(c, carry, *, qv, kv, v_s, o_s, st_ref, sn_ref, gt_s):
    rows = _block_rows(c, CHUNK)
    gt = gt_s[_block_rows(c, SUB), :]
    v = [v_s[rows, p * PAIR:(p + 1) * PAIR] for p in range(HEADS // 2)]

    ti = lax.broadcasted_iota(jnp.int32, (CHUNK, 2 * CHUNK), 0)
    si = lax.broadcasted_iota(jnp.int32, (CHUNK, 2 * CHUNK), 1) % CHUNK
    bt, bs = ti // SUB, si // SUB
    masks = {
        "d": (bt == bs) & (si <= ti),
        "8": (bt == bs + 1) & (bt % 2 == 1),
        "16": (bt // 4 == bs // 4) & ((bt // 2) % 2 == 1) & ((bs // 2) % 2 == 0),
        "32": (bt // 4 == 1) & (bs // 4 == 0),
    }

    def bdiag(x):
        z = jnp.zeros((x.shape[0], HEAD_DIM), x.dtype)
        return jnp.concatenate(
            [jnp.concatenate([x[:, :HEAD_DIM], z], axis=1),
             jnp.concatenate([z, x[:, HEAD_DIM:]], axis=1)], axis=0)

    pairs = [slice(p * PAIR, (p + 1) * PAIR) for p in range(HEADS // 2)]
    scores = [{name: lax.dot_general(qv[name][rows, ps], bdiag(kv[name][rows, ps]), NT,
                                     preferred_element_type=F32) for name in LEVELS}
              for ps in pairs]
    states = [(st_ref[2 * p], st_ref[2 * p + 1]) for p in range(len(pairs))]
    inter = [jnp.dot(qv["c"][rows, ps], sn_ref[p], preferred_element_type=F32)
             for p, ps in enumerate(pairs)]
    upds = [lax.dot_general(jnp.concatenate([v[p][:, :HEAD_DIM], v[p][:, HEAD_DIM:]], axis=0),
                            bdiag(kv["c"][rows, ps]), TN, preferred_element_type=F32)
            for p, ps in enumerate(pairs)]
    for p, ps in enumerate(pairs):
        pm = jnp.zeros((CHUNK, 2 * CHUNK), F32)
        for name in LEVELS:
            pm = jnp.where(masks[name], scores[p][name], pm)
        o_s[rows, ps] = inter[p] + jnp.dot(pm.astype(BF16), bdiag(v[p]),
                                           preferred_element_type=F32)
        g_p = jnp.tile(gt[:, ps], (HEAD_DIM // SUB, 1))
        st0, st1 = states[p]
        st0 = st0 * g_p[:, :HEAD_DIM] + upds[p][:, :HEAD_DIM]
        st1 = st1 * g_p[:, HEAD_DIM:] + upds[p][:, HEAD_DIM:]
        st_ref[2 * p], st_ref[2 * p + 1] = st0, st1
        sn_ref[p, :HEAD_DIM, :HEAD_DIM] = st0.T.astype(BF16)
        sn_ref[p, HEAD_DIM:, HEAD_DIM:] = st1.T.astype(BF16)
    return carry


def _conv_block(r, carry, *, ush_ref, cw_ref, cb_ref, y_s, rows_per):
    r0 = r * rows_per
    nsub = rows_per // SUB
    lead = TAIL - CONV_K + 1
    for lo in range(0, D_MODEL, CONV_LANES):
        lanes = slice(lo, lo + CONV_LANES)
        acc = [cb_ref[:, lanes]] * nsub
        for shift in range(SUB):
            taps = [j for j in range(CONV_K) if (lead + j) % SUB == shift]
            w = {j: cw_ref[j, :, lanes] for j in taps}
            first = min((lead + j) // SUB for j in taps)
            last = max((lead + j) // SUB for j in taps) + nsub - 1
            for m in range(first, last + 1):
                blk = ush_ref[shift, _aligned(r0 + (m + 1) * SUB, SUB), lanes]
                for j in taps:
                    b = m - (lead + j) // SUB
                    if 0 <= b < nsub:
                        acc[b] = acc[b] + blk * w[j]
        for b in range(nsub):
            y_s[_aligned(r0 + b * SUB, SUB), lanes] = acc[b]
    return carry


def _mixer_body(x_ref, xp_ref, mixn_ref, win_ref, lbl_ref, hn_ref, wo_ref, cw_ref, cb_ref, lng_ref,
                lnb_ref, wpw_ref, bpw_ref, wout_ref, o_ref,
                st_ref, sn_ref, tail_ref, ush_ref, gt_s, h_s, hp_s, v_s, q_s, k_s, lf_s, o_s, y_s,
                *var_refs, layer):
    tm = x_ref.shape[1]
    names = LEVELS + ("c",)
    qv = dict(zip(names, var_refs[:len(names)]))
    kv = dict(zip(names, var_refs[len(names):]))

    @pl.when(pl.program_id(1) == 0)
    def _():
        st_ref[...] = jnp.zeros_like(st_ref)
        sn_ref[...] = jnp.zeros_like(sn_ref)
        tail_ref[...] = jnp.zeros_like(tail_ref)
        h_s[...] = jnp.zeros_like(h_s)
        o_s[...] = jnp.zeros_like(o_s)
        y_s[...] = jnp.zeros_like(y_s)

    hp_s[...] = h_s[...]
    x = x_ref[0]
    h_s[...] = _rms(x, mixn_ref[...]).astype(BF16)

    def proj(i, src=h_s):
        return jnp.dot(src[...], win_ref[:, i * D_MODEL:(i + 1) * D_MODEL],
                       preferred_element_type=F32)

    lbl = lbl_ref[...]
    e = jnp.exp(lbl - jnp.max(lbl, axis=0, keepdims=True))
    lb = jnp.sum(e[0:layer + 1], axis=0, keepdims=True) / jnp.sum(e, axis=0, keepdims=True)

    u = proj(4) * _sigmoid(proj(5))
    ext = jnp.concatenate([tail_ref[...], u], axis=0)
    n_ext = TAIL + tm
    tiles = [ext[i:i + SUB] for i in range(0, n_ext, SUB)]
    rid = lax.broadcasted_iota(jnp.int32, (SUB, D_MODEL), 0)
    ush_ref[0, SUB:SUB + n_ext, :] = ext
    rot = tiles
    for r in range(SUB - 1, 0, -1):
        rot = [pltpu.roll(t, 1, 0) for t in rot]
        ush_ref[r, SUB:SUB + n_ext, :] = jnp.concatenate(
            [jnp.where(rid < SUB - r, rot[i], rot[(i + 1) % len(rot)]) for i in range(len(rot))],
            axis=0)
    tail_ref[...] = u[tm - TAIL:tm]

    o = o_s[...]
    hn = hn_ref[...]
    o = jnp.concatenate(
        [_rms(o[:, i * HEAD_DIM:(i + 1) * HEAD_DIM], hn) for i in range(HEADS)], axis=1)
    g_out = proj(3, hp_s)
    y_a = jnp.dot((o * (g_out * _sigmoid(g_out))).astype(BF16), wo_ref[...],
                  preferred_element_type=F32)

    f = lb + (1.0 - lb) * _sigmoid(proj(1))
    f = jnp.clip(f, 1e-6, 1.0)
    lf_s[...] = jnp.log(f)
    k_s[...] = 1.0 - f
    q_s[...] = proj(0)
    v_s[...] = proj(2).astype(BF16)

    u = y_s[...]
    mu = jnp.mean(u, axis=-1, keepdims=True)
    uc = u - mu
    var = jnp.mean(uc * uc, axis=-1, keepdims=True)
    u = uc * lax.rsqrt(var + EPS) * lng_ref[...] + lnb_ref[...]
    u = u * _sigmoid(u)
    y_b = jnp.dot(u.astype(BF16), wpw_ref[...], preferred_element_type=F32) + bpw_ref[...]
    merged = _sigmoid(proj(6, hp_s)) * y_a + _sigmoid(proj(7, hp_s)) * y_b

    n_chunks = tm // CHUNK
    for c in range(n_chunks):
        if c == 0:
            _hgrn_prep(0, 0, q_s=q_s, k_s=k_s, lf_s=lf_s, qv=qv, kv=kv, gt_s=gt_s)
        _hgrn_mm(c, 0, qv=qv, kv=kv, v_s=v_s, o_s=o_s, st_ref=st_ref, sn_ref=sn_ref, gt_s=gt_s)
        if c + 1 < n_chunks:
            _hgrn_prep(c + 1, 0, q_s=q_s, k_s=k_s, lf_s=lf_s, qv=qv, kv=kv, gt_s=gt_s)

    o_ref[0] = xp_ref[0] + jnp.dot(merged.astype(BF16), wout_ref[...],
                                   preferred_element_type=F32)

    lax.fori_loop(0, tm // CONV_ROWS,
                  functools.partial(_conv_block, ush_ref=ush_ref, cw_ref=cw_ref, cb_ref=cb_ref,
                                    y_s=y_s, rows_per=CONV_ROWS), 0)


def _mixer(x, mixn, win, lbl, hn, wo, cw, cb, lng, lnb, wpw, bpw, wout, *, layer):
    b, l, d = x.shape
    tm = MIX_TM
    act = lambda: pltpu.VMEM((tm, d), F32)
    act16 = lambda: pltpu.VMEM((tm, d), BF16)
    n_var = 2 * (len(LEVELS) + 1)
    n_tiles = l // tm
    return pl.pallas_call(
        functools.partial(_mixer_body, layer=layer),
        grid=(b, n_tiles + 1),
        in_specs=[
            pl.BlockSpec((1, tm, d), lambda i, j: (i, jnp.minimum(j, n_tiles - 1), 0)),
            pl.BlockSpec((1, tm, d), lambda i, j: (i, jnp.maximum(j - 1, 0), 0)),
            _resident(mixn.shape), _resident(win.shape), _resident(lbl.shape),
            _resident(hn.shape), _resident(wo.shape), _resident(cw.shape), _resident(cb.shape),
            _resident(lng.shape), _resident(lnb.shape), _resident(wpw.shape),
            _resident(bpw.shape), _resident(wout.shape),
        ],
        out_specs=pl.BlockSpec((1, tm, d), lambda i, j: (i, jnp.maximum(j - 1, 0), 0)),
        out_shape=jax.ShapeDtypeStruct((b, l, d), F32),
        scratch_shapes=[
            pltpu.VMEM((HEADS, HEAD_DIM, HEAD_DIM), F32),
            pltpu.VMEM((HEADS // 2, PAIR, PAIR), BF16),
            pltpu.VMEM((TAIL, d), F32),
            pltpu.VMEM((SUB, SUB + TAIL + tm, d), F32),
            pltpu.VMEM((tm // CHUNK * SUB, d), F32),
            act16(), act16(),
            act16(),
            act(), act(), act(), act(), act(),
        ] + [act16() for _ in range(n_var)],
        compiler_params=pltpu.CompilerParams(
            dimension_semantics=("arbitrary", "arbitrary"), vmem_limit_bytes=VMEM_LIMIT),
        name="mixer",
    )(x, x, mixn, win, lbl, hn, wo, cw, cb, lng, lnb, wpw, bpw, wout)


def kernel(x, ffn1_norm, ffn1_w_gate, ffn1_w_up, ffn1_w_down, mix_norm, w_in, hgrn_lb_logits, hgrn_head_norm, hgrn_w_o, conv_w, conv_b, conv_ln_g, conv_ln_b, conv_w_pw, conv_b_pw, w_out, ffn2_norm, ffn2_w_gate, ffn2_w_up, ffn2_w_down, final_norm):
    b, l, d = x.shape
    depth = ffn1_norm.shape[0]
    row = lambda a: a.reshape(1, -1)
    w16 = lambda a: a.astype(BF16)
    sub8 = lambda a: jnp.broadcast_to(a[..., None, :], a.shape[:-1] + (SUB, a.shape[-1]))
    fin = row(final_norm)
    for i in range(depth):
        x = _ffn(x.reshape(b * l, d), row(ffn1_norm[i]), ffn1_w_gate, ffn1_w_up, ffn1_w_down, fin,
                 layer=i, final=False).reshape(b, l, d)
        x = _mixer(x, row(mix_norm[i]), w16(w_in[i]), hgrn_lb_logits, row(hgrn_head_norm[i]),
                   w16(hgrn_w_o[i]), sub8(conv_w[i]), sub8(conv_b[i]), row(conv_ln_g[i]),
                   row(conv_ln_b[i]), w16(conv_w_pw[i]), row(conv_b_pw[i]), w16(w_out[i]), layer=i)
        last = i == depth - 1
        x = _ffn(x.reshape(b * l, d), row(ffn2_norm[i]), ffn2_w_gate, ffn2_w_up, ffn2_w_down, fin,
                 layer=i, final=last).reshape(b, l, d)
    return x
```

```python
import functools

import jax
import jax.numpy as jnp
from jax import lax
from jax.experimental import pallas as pl
from jax.experimental.pallas import tpu as pltpu

D_MODEL = 1024
D_FF = 2816
HEADS = 8
HEAD_DIM = 128
PAIR = 2 * HEAD_DIM
CONV_K = 31
FFN_RES = 0.5
EPS = 1e-6
N_SPLITS = 8

CHUNK = 64
SUB = 8
NBLK = CHUNK // SUB
TAIL = 32
MID = 3
CONV_ROWS = 128
CONV_LANES = 256
LEVELS = ("d", "8", "16", "32")

FFN_TM = 1024
FFN_ROWS = 256
FFN_COLS = 1536
MIX_TM = 256
VMEM_LIMIT = 60 * 1024 * 1024
VMEM_LIMIT_CAST = 62 * 1024 * 1024

F32 = jnp.float32
BF16 = jnp.bfloat16
NT = (((1,), (1,)), ((), ()))
TN = (((0,), (0,)), ((), ()))


def _rms(x, g):
    ms = jnp.mean(x * x, axis=-1, keepdims=True)
    return x * lax.rsqrt(ms + EPS) * g


def _sigmoid(x):
    return 0.5 * jnp.tanh(0.5 * x) + 0.5


def _aligned(start, n):
    return pl.ds(start if isinstance(start, int) else pl.multiple_of(start, n), n)


def _block_rows(i, n):
    return _aligned(i * n, n)


def _resident(shape):
    nd = len(shape)
    return pl.BlockSpec(shape, lambda *_: (0,) * nd, pipeline_mode=pl.Buffered(1))


def _ffn_body(x_ref, nrm_ref, wg_ref, wu_ref, wd_ref, fin_ref, *rest, final):
    n_cast = (len(rest) - 1) // 2
    o_ref = rest[n_cast]
    for src, dst in zip(rest[:n_cast], rest[n_cast + 1:]):
        dst[...] = src[...].astype(BF16)

    f = wg_ref.shape[1]
    for r0 in range(0, x_ref.shape[0], FFN_ROWS):
        rows = slice(r0, r0 + FFN_ROWS)
        x = x_ref[rows, :]
        h = _rms(x, nrm_ref[...])
        y = x
        for lo in range(0, f, FFN_COLS):
            cols = slice(lo, min(lo + FFN_COLS, f))
            g = jnp.dot(h, wg_ref[:, cols], preferred_element_type=F32)
            u = jnp.dot(h, wu_ref[:, cols], preferred_element_type=F32)
            y = y + FFN_RES * jnp.dot(g * _sigmoid(g) * u, wd_ref[cols, :],
                                      preferred_element_type=F32)
        if final:
            y = _rms(y, fin_ref[...])
        o_ref[rows, :] = y


def _ffn(x2d, nrm, wg, wu, wd, fin, *, layer, final, cast=()):
    t, d = x2d.shape
    f = wg.shape[2]
    tm = FFN_TM
    steps = t // tm
    slab = lambda a: a.shape[1] // steps
    assert all(a.shape[1] % (steps * 16) == 0 for a in cast)
    weight = lambda r, c: pl.BlockSpec((None, r, c), lambda i: (layer, 0, 0),
                                       pipeline_mode=pl.Buffered(1))
    return pl.pallas_call(
        functools.partial(_ffn_body, final=final),
        grid=(t // tm,),
        in_specs=[
            pl.BlockSpec((tm, d), lambda i: (i, 0)),
            _resident((1, d)),
            weight(d, f),
            weight(d, f),
            weight(f, d),
            _resident((1, d)),
        ] + [pl.BlockSpec((None, slab(a), a.shape[2]), lambda i: (layer, i, 0)) for a in cast],
        out_specs=[pl.BlockSpec((tm, d), lambda i: (i, 0))]
        + [pl.BlockSpec((slab(a), a.shape[2]), lambda i: (i, 0)) for a in cast],
        out_shape=[jax.ShapeDtypeStruct((t, d), F32)]
        + [jax.ShapeDtypeStruct(a.shape[1:], BF16) for a in cast],
        compiler_params=pltpu.CompilerParams(
            dimension_semantics=("arbitrary",),
            vmem_limit_bytes=VMEM_LIMIT_CAST if cast else VMEM_LIMIT),
        name="ffn_final" if final else "ffn",
    )(x2d, nrm, wg, wu, wd, fin, *cast)


def _hgrn_prep(c, carry, *, q_s, k_s, lf_s, qv, kv, gt_s):
    rows = _block_rows(c, CHUNK)
    lf, q, k = lf_s[rows, :], q_s[rows, :], k_s[rows, :]
    rid = lax.broadcasted_iota(jnp.int32, (SUB, D_MODEL), 0)

    qe, kf, qd, kd, g = [], [], [], [], []
    for i in range(NBLK):
        sl = slice(i * SUB, (i + 1) * SUB)
        a = lf[sl]
        for s in (1, 2, 4):
            a = a + jnp.where(rid >= s, pltpu.roll(a, s, 0), 0.0)
        tot = jnp.broadcast_to(a[SUB - 1:SUB], (SUB, D_MODEL))
        dm = a - jnp.broadcast_to(a[MID:MID + 1], (SUB, D_MODEL))
        qe.append(q[sl] * jnp.exp(a))
        kf.append(k[sl] * jnp.exp(tot - a))
        qd.append(q[sl] * jnp.exp(dm))
        kd.append(k[sl] * jnp.exp(-dm))
        g.append(jnp.exp(tot))

    def running(blocks):
        out, acc = [None], None
        for blk in blocks:
            acc = blk if acc is None else acc * blk
            out.append(acc)
        return out

    def scaled(base, factors):
        return [b if f is None else b * f for b, f in zip(base, factors)]

    def store(ref, blocks):
        ref[rows, :] = jnp.concatenate(blocks, axis=0).astype(BF16)

    def q_factors(nb):
        out = []
        for lo in range(0, NBLK, nb):
            out += running(g[lo:lo + nb])[:nb]
        return out

    def k_factors(nb):
        out = []
        for lo in range(0, NBLK, nb):
            out += running(g[lo:lo + nb][::-1])[:nb][::-1]
        return out

    store(qv["d"], qd)
    store(kv["d"], kd)
    store(qv["8"], qe)
    store(kv["8"], kf)
    for name, nb in (("16", 2), ("32", 4), ("c", NBLK)):
        store(qv[name], scaled(qe, q_factors(nb)))
        store(kv[name], scaled(kf, k_factors(nb)))
    gt_s[_block_rows(c, SUB), :] = running(g)[NBLK]
    return carry


def _hgrn_mm(c, carry, *, qv, kv, v_s, o_s, st_ref, sn_ref, gt_s):
    rows = _block_rows(c, CHUNK)
    gt = gt_s[_block_rows(c, SUB), :]
    v = [v_s[rows, p * PAIR:(p + 1) * PAIR] for p in range(HEADS // 2)]

    ti = lax.broadcasted_iota(jnp.int32, (CHUNK, 2 * CHUNK), 0)
    si = lax.broadcasted_iota(jnp.int32, (CHUNK, 2 * CHUNK), 1) % CHUNK
    bt, bs = ti // SUB, si // SUB
    masks = {
        "d": (bt == bs) & (si <= ti),
        "8": (bt == bs + 1) & (bt % 2 == 1),
        "16": (bt // 4 == bs // 4) & ((bt // 2) % 2 == 1) & ((bs // 2) % 2 == 0),
        "32": (bt // 4 == 1) & (bs // 4 == 0),
    }

    def bdiag(x):
        z = jnp.zeros((x.shape[0], HEAD_DIM), x.dtype)
        return jnp.concatenate(
            [jnp.concatenate([x[:, :HEAD_DIM], z], axis=1),
             jnp.concatenate([z, x[:, HEAD_DIM:]], axis=1)], axis=0)

    pairs = [slice(p * PAIR, (p + 1) * PAIR) for p in range(HEADS // 2)]
    scores = [{name: lax.dot_general(qv[name][rows, ps], bdiag(kv[name][rows, ps]), NT,
                                     preferred_element_type=F32) for name in LEVELS}
              for ps in pairs]
    states = [(st_ref[2 * p], st_ref[2 * p + 1]) for p in range(len(pairs))]
    inter = [jnp.dot(qv["c"][rows, ps], sn_ref[p], preferred_element_type=F32)
             for p, ps in enumerate(pairs)]
    upds = [lax.dot_general(jnp.concatenate([v[p][:, :HEAD_DIM], v[p][:, HEAD_DIM:]], axis=0),
                            bdiag(kv["c"][rows, ps]), TN, preferred_element_type=F32)
            for p, ps in enumerate(pairs)]
    for p, ps in enumerate(pairs):
        pm = jnp.zeros((CHUNK, 2 * CHUNK), F32)
        for name in LEVELS:
            pm = jnp.where(masks[name], scores[p][name], pm)
        o_s[rows, ps] = inter[p] + jnp.dot(pm.astype(BF16), bdiag(v[p]),
                                           preferred_element_type=F32)
        g_p = jnp.tile(gt[:, ps], (HEAD_DIM // SUB, 1))
        st0, st1 = states[p]
        st0 = st0 * g_p[:, :HEAD_DIM] + upds[p][:, :HEAD_DIM]
        st1 = st1 * g_p[:, HEAD_DIM:] + upds[p][:, HEAD_DIM:]
        st_ref[2 * p], st_ref[2 * p + 1] = st0, st1
        sn_ref[p, :HEAD_DIM, :HEAD_DIM] = st0.T.astype(BF16)
        sn_ref[p, HEAD_DIM:, HEAD_DIM:] = st1.T.astype(BF16)
    return carry


def _conv_block(r, carry, *, ush_ref, cw_ref, cb_ref, y_s, rows_per):
    r0 = r * rows_per
    nsub = rows_per // SUB
    lead = TAIL - CONV_K + 1
    for lo in range(0, D_MODEL, CONV_LANES):
        lanes = slice(lo, lo + CONV_LANES)
        acc = [cb_ref[:, lanes]] * nsub
        for shift in range(SUB):
            taps = [j for j in range(CONV_K) if (lead + j) % SUB == shift]
            w = {j: cw_ref[j, :, lanes] for j in taps}
            first = min((lead + j) // SUB for j in taps)
            last = max((lead + j) // SUB for j in taps) + nsub - 1
            for m in range(first, last + 1):
                blk = ush_ref[shift, _aligned(r0 + (m + 1) * SUB, SUB), lanes]
                for j in taps:
                    b = m - (lead + j) // SUB
                    if 0 <= b < nsub:
                        acc[b] = acc[b] + blk * w[j]
        for b in range(nsub):
            y_s[_aligned(r0 + b * SUB, SUB), lanes] = acc[b]
    return carry


def _mixer_body(x_ref, xp_ref, mixn_ref, win_ref, lbl_ref, hn_ref, wo_ref, cw_ref, cb_ref, lng_ref,
                lnb_ref, wpw_ref, bpw_ref, wout_ref, o_ref,
                st_ref, sn_ref, tail_ref, ush_ref, gt_s, h_s, hp_s, v_s, q_s, k_s, lf_s, o_s, y_s,
                *var_refs, layer, tiles_per_seq):
    tm = x_ref.shape[0]
    names = LEVELS + ("c",)
    qv = dict(zip(names, var_refs[:len(names)]))
    kv = dict(zip(names, var_refs[len(names):]))

    step = pl.program_id(0)

    @pl.when(step % tiles_per_seq == 0)
    def _():
        st_ref[...] = jnp.zeros_like(st_ref)
        sn_ref[...] = jnp.zeros_like(sn_ref)
        tail_ref[...] = jnp.zeros_like(tail_ref)

    @pl.when(step == 0)
    def _():
        h_s[...] = jnp.zeros_like(h_s)
        o_s[...] = jnp.zeros_like(o_s)
        y_s[...] = jnp.zeros_like(y_s)

    hp_s[...] = h_s[...]
    x = x_ref[...]
    h_s[...] = _rms(x, mixn_ref[...]).astype(BF16)

    def proj(i, src=h_s):
        return jnp.dot(src[...], win_ref[:, i * D_MODEL:(i + 1) * D_MODEL],
                       preferred_element_type=F32)

    lbl = lbl_ref[...]
    e = jnp.exp(lbl - jnp.max(lbl, axis=0, keepdims=True))
    lb = jnp.sum(e[0:layer + 1], axis=0, keepdims=True) / jnp.sum(e, axis=0, keepdims=True)

    u = proj(4) * _sigmoid(proj(5))
    ext = jnp.concatenate([tail_ref[...], u], axis=0)
    n_ext = TAIL + tm
    tiles = [ext[i:i + SUB] for i in range(0, n_ext, SUB)]
    rid = lax.broadcasted_iota(jnp.int32, (SUB, D_MODEL), 0)
    ush_ref[0, SUB:SUB + n_ext, :] = ext
    rot = tiles
    for r in range(SUB - 1, 0, -1):
        rot = [pltpu.roll(t, 1, 0) for t in rot]
        ush_ref[r, SUB:SUB + n_ext, :] = jnp.concatenate(
            [jnp.where(rid < SUB - r, rot[i], rot[(i + 1) % len(rot)]) for i in range(len(rot))],
            axis=0)
    tail_ref[...] = u[tm - TAIL:tm]

    o = o_s[...]
    hn = hn_ref[...]
    o = jnp.concatenate(
        [_rms(o[:, i * HEAD_DIM:(i + 1) * HEAD_DIM], hn) for i in range(HEADS)], axis=1)
    g_out = proj(3, hp_s)
    y_a = jnp.dot((o * (g_out * _sigmoid(g_out))).astype(BF16), wo_ref[...],
                  preferred_element_type=F32)

    f = lb + (1.0 - lb) * _sigmoid(proj(1))
    f = jnp.clip(f, 1e-6, 1.0)
    lf_s[...] = jnp.log(f)
    k_s[...] = 1.0 - f
    q_s[...] = proj(0)
    v_s[...] = proj(2).astype(BF16)

    u = y_s[...]
    mu = jnp.mean(u, axis=-1, keepdims=True)
    uc = u - mu
    var = jnp.mean(uc * uc, axis=-1, keepdims=True)
    u = uc * lax.rsqrt(var + EPS) * lng_ref[...] + lnb_ref[...]
    u = u * _sigmoid(u)
    y_b = jnp.dot(u.astype(BF16), wpw_ref[...], preferred_element_type=F32) + bpw_ref[...]
    merged = _sigmoid(proj(6, hp_s)) * y_a + _sigmoid(proj(7, hp_s)) * y_b

    o_ref[...] = xp_ref[...] + jnp.dot(merged.astype(BF16), wout_ref[...],
                                       preferred_element_type=F32)

    n_chunks = tm // CHUNK
    for c in range(n_chunks):
        if c == 0:
            _hgrn_prep(0, 0, q_s=q_s, k_s=k_s, lf_s=lf_s, qv=qv, kv=kv, gt_s=gt_s)
        _hgrn_mm(c, 0, qv=qv, kv=kv, v_s=v_s, o_s=o_s, st_ref=st_ref, sn_ref=sn_ref, gt_s=gt_s)
        if c + 1 < n_chunks:
            _hgrn_prep(c + 1, 0, q_s=q_s, k_s=k_s, lf_s=lf_s, qv=qv, kv=kv, gt_s=gt_s)

    lax.fori_loop(0, tm // CONV_ROWS,
                  functools.partial(_conv_block, ush_ref=ush_ref, cw_ref=cw_ref, cb_ref=cb_ref,
                                    y_s=y_s, rows_per=CONV_ROWS), 0)


def _mixer(x, mixn, win, lbl, hn, wo, cw, cb, lng, lnb, wpw, bpw, wout, *, layer):
    b, l, d = x.shape
    x2d = x.reshape(b * l, d)
    tm = MIX_TM
    assert l % tm == 0
    act = lambda: pltpu.VMEM((tm, d), F32)
    act16 = lambda: pltpu.VMEM((tm, d), BF16)
    n_var = 2 * (len(LEVELS) + 1)
    n_tiles = b * l // tm
    return pl.pallas_call(
        functools.partial(_mixer_body, layer=layer, tiles_per_seq=l // tm),
        grid=(n_tiles + 1,),
        in_specs=[
            pl.BlockSpec((tm, d), lambda s: (jnp.minimum(s, n_tiles - 1), 0)),
            pl.BlockSpec((tm, d), lambda s: (jnp.maximum(s - 1, 0), 0)),
            _resident(mixn.shape), _resident(win.shape), _resident(lbl.shape),
            _resident(hn.shape), _resident(wo.shape), _resident(cw.shape), _resident(cb.shape),
            _resident(lng.shape), _resident(lnb.shape), _resident(wpw.shape),
            _resident(bpw.shape), _resident(wout.shape),
        ],
        out_specs=pl.BlockSpec((tm, d), lambda s: (jnp.maximum(s - 1, 0), 0)),
        out_shape=jax.ShapeDtypeStruct((b * l, d), F32),
        scratch_shapes=[
            pltpu.VMEM((HEADS, HEAD_DIM, HEAD_DIM), F32),
            pltpu.VMEM((HEADS // 2, PAIR, PAIR), BF16),
            pltpu.VMEM((TAIL, d), F32),
            pltpu.VMEM((SUB, SUB + TAIL + tm, d), F32),
            pltpu.VMEM((tm // CHUNK * SUB, d), F32),
            act16(), act16(),
            act16(),
            act(), act(), act(), act(), act(),
        ] + [act16() for _ in range(n_var)],
        compiler_params=pltpu.CompilerParams(
            dimension_semantics=("arbitrary",), vmem_limit_bytes=VMEM_LIMIT),
        name="mixer",
    )(x2d, x2d, mixn, win, lbl, hn, wo, cw, cb, lng, lnb, wpw, bpw, wout).reshape(b, l, d)


def kernel(x, ffn1_norm, ffn1_w_gate, ffn1_w_up, ffn1_w_down, mix_norm, w_in, hgrn_lb_logits, hgrn_head_norm, hgrn_w_o, conv_w, conv_b, conv_ln_g, conv_ln_b, conv_w_pw, conv_b_pw, w_out, ffn2_norm, ffn2_w_gate, ffn2_w_up, ffn2_w_down, final_norm):
    b, l, d = x.shape
    depth = ffn1_norm.shape[0]
    row = lambda a: a.reshape(1, -1)
    sub8 = lambda a: jnp.broadcast_to(a[..., None, :], a.shape[:-1] + (SUB, a.shape[-1]))
    fin = row(final_norm)
    for i in range(depth):
        x, win16, wo16, wpw16, wout16 = _ffn(
            x.reshape(b * l, d), row(ffn1_norm[i]), ffn1_w_gate, ffn1_w_up, ffn1_w_down, fin,
            layer=i, final=False, cast=(w_in, hgrn_w_o, conv_w_pw, w_out))
        x = _mixer(x.reshape(b, l, d), row(mix_norm[i]), win16, hgrn_lb_logits,
                   row(hgrn_head_norm[i]), wo16, sub8(conv_w[i]), sub8(conv_b[i]),
                   row(conv_ln_g[i]), row(conv_ln_b[i]), wpw16, row(conv_b_pw[i]), wout16, layer=i)
        last = i == depth - 1
        x, = _ffn(x.reshape(b * l, d), row(ffn2_norm[i]), ffn2_w_gate, ffn2_w_up, ffn2_w_down, fin,
                  layer=i, final=last)
        x = x.reshape(b, l, d)
    return x
```

```python
import functools

import jax
import jax.numpy as jnp
from jax import lax
from jax.experimental import pallas as pl
from jax.experimental.pallas import tpu as pltpu

D_MODEL = 1024
D_FF = 2816
HEADS = 8
HEAD_DIM = 128
PAIR = 2 * HEAD_DIM
CONV_K = 31
FFN_RES = 0.5
EPS = 1e-6
N_SPLITS = 8

CHUNK = 64
SUB = 8
NBLK = CHUNK // SUB
TAIL = 32
MID = 3
CONV_ROWS = 128
CONV_LANES = 256
LEVELS = ("d", "8", "16", "32")

FFN_TM = 1024
FFN_ROWS = 256
FFN_COLS = 1536
MIX_TM = 256
VMEM_LIMIT = 60 * 1024 * 1024
VMEM_LIMIT_CAST = 62 * 1024 * 1024

F32 = jnp.float32
BF16 = jnp.bfloat16
NT = (((1,), (1,)), ((), ()))
TN = (((0,), (0,)), ((), ()))


def _rms(x, g):
    ms = jnp.mean(x * x, axis=-1, keepdims=True)
    return x * lax.rsqrt(ms + EPS) * g


def _sigmoid(x):
    return 0.5 * jnp.tanh(0.5 * x) + 0.5


def _aligned(start, n):
    return pl.ds(start if isinstance(start, int) else pl.multiple_of(start, n), n)


def _block_rows(i, n):
    return _aligned(i * n, n)


def _resident(shape):
    nd = len(shape)
    return pl.BlockSpec(shape, lambda *_: (0,) * nd, pipeline_mode=pl.Buffered(1))


def _ffn_body(x_ref, nrm_ref, wg_ref, wu_ref, wd_ref, fin_ref, *rest, final):
    n_cast = (len(rest) - 1) // 2
    o_ref = rest[n_cast]
    for src, dst in zip(rest[:n_cast], rest[n_cast + 1:]):
        dst[...] = src[...].astype(BF16)

    f = wg_ref.shape[1]
    for r0 in range(0, x_ref.shape[0], FFN_ROWS):
        rows = slice(r0, r0 + FFN_ROWS)
        x = x_ref[rows, :]
        h = _rms(x, nrm_ref[...])
        y = x
        for lo in range(0, f, FFN_COLS):
            cols = slice(lo, min(lo + FFN_COLS, f))
            g = jnp.dot(h, wg_ref[:, cols], preferred_element_type=F32)
            u = jnp.dot(h, wu_ref[:, cols], preferred_element_type=F32)
            y = y + FFN_RES * jnp.dot(g * _sigmoid(g) * u, wd_ref[cols, :],
                                      preferred_element_type=F32)
        if final:
            y = _rms(y, fin_ref[...])
        o_ref[rows, :] = y


def _ffn(x2d, nrm, wg, wu, wd, fin, *, layer, final, cast=()):
    t, d = x2d.shape
    f = wg.shape[2]
    tm = FFN_TM
    steps = t // tm
    slab = lambda a: a.shape[1] // steps
    assert all(a.shape[1] % (steps * 16) == 0 for a in cast)
    weight = lambda r, c: pl.BlockSpec((None, r, c), lambda i: (layer, 0, 0),
                                       pipeline_mode=pl.Buffered(1))
    return pl.pallas_call(
        functools.partial(_ffn_body, final=final),
        grid=(t // tm,),
        in_specs=[
            pl.BlockSpec((tm, d), lambda i: (i, 0)),
            _resident((1, d)),
            weight(d, f),
            weight(d, f),
            weight(f, d),
            _resident((1, d)),
        ] + [pl.BlockSpec((None, slab(a), a.shape[2]), lambda i: (layer, i, 0)) for a in cast],
        out_specs=[pl.BlockSpec((tm, d), lambda i: (i, 0))]
        + [pl.BlockSpec((slab(a), a.shape[2]), lambda i: (i, 0)) for a in cast],
        out_shape=[jax.ShapeDtypeStruct((t, d), F32)]
        + [jax.ShapeDtypeStruct(a.shape[1:], BF16) for a in cast],
        compiler_params=pltpu.CompilerParams(
            dimension_semantics=("arbitrary",),
            vmem_limit_bytes=VMEM_LIMIT_CAST if cast else VMEM_LIMIT),
        name="ffn_final" if final else "ffn",
    )(x2d, nrm, wg, wu, wd, fin, *cast)


def _hgrn_prep(c, carry, *, q_s, k_s, lf_s, qv, kv, gt_s):
    rows = _block_rows(c, CHUNK)
    lf, q, k = lf_s[rows, :], q_s[rows, :], k_s[rows, :]
    rid = lax.broadcasted_iota(jnp.int32, (SUB, D_MODEL), 0)

    qe, kf, qd, kd, g = [], [], [], [], []
    for i in range(NBLK):
        sl = slice(i * SUB, (i + 1) * SUB)
        a = lf[sl]
        for s in (1, 2, 4):
            a = a + jnp.where(rid >= s, pltpu.roll(a, s, 0), 0.0)
        tot = jnp.broadcast_to(a[SUB - 1:SUB], (SUB, D_MODEL))
        dm = a - jnp.broadcast_to(a[MID:MID + 1], (SUB, D_MODEL))
        qe.append(q[sl] * jnp.exp(a))
        kf.append(k[sl] * jnp.exp(tot - a))
        qd.append(q[sl] * jnp.exp(dm))
        kd.append(k[sl] * jnp.exp(-dm))
        g.append(jnp.exp(tot))

    def running(blocks):
        out, acc = [None], None
        for blk in blocks:
            acc = blk if acc is None else acc * blk
            out.append(acc)
        return out

    def scaled(base, factors):
        return [b if f is None else b * f for b, f in zip(base, factors)]

    def store(ref, blocks):
        ref[rows, :] = jnp.concatenate(blocks, axis=0).astype(BF16)

    def q_factors(nb):
        out = []
        for lo in range(0, NBLK, nb):
            out += running(g[lo:lo + nb])[:nb]
        return out

    def k_factors(nb):
        out = []
        for lo in range(0, NBLK, nb):
            out += running(g[lo:lo + nb][::-1])[:nb][::-1]
        return out

    store(qv["d"], qd)
    store(kv["d"], kd)
    store(qv["8"], qe)
    store(kv["8"], kf)
    for name, nb in (("16", 2), ("32", 4), ("c", NBLK)):
        store(qv[name], scaled(qe, q_factors(nb)))
        store(kv[name], scaled(kf, k_factors(nb)))
    gt_s[_block_rows(c, SUB), :] = running(g)[NBLK]
    return carry


def _hgrn_mm(c, carry, *, qv, kv, v_s, o_s, st_ref, sn_ref, gt_s):
    rows = _block_rows(c, CHUNK)
    gt = gt_s[_block_rows(c, SUB), :]
    v = [v_s[rows, p * PAIR:(p + 1) * PAIR] for p in range(HEADS // 2)]

    ti = lax.broadcasted_iota(jnp.int32, (CHUNK, 2 * CHUNK), 0)
    si = lax.broadcasted_iota(jnp.int32, (CHUNK, 2 * CHUNK), 1) % CHUNK
    bt, bs = ti // SUB, si // SUB
    masks = {
        "d": (bt == bs) & (si <= ti),
        "8": (bt == bs + 1) & (bt % 2 == 1),
        "16": (bt // 4 == bs // 4) & ((bt // 2) % 2 == 1) & ((bs // 2) % 2 == 0),
        "32": (bt // 4 == 1) & (bs // 4 == 0),
    }

    def bdiag(x):
        z = jnp.zeros((x.shape[0], HEAD_DIM), x.dtype)
        return jnp.concatenate(
            [jnp.concatenate([x[:, :HEAD_DIM], z], axis=1),
             jnp.concatenate([z, x[:, HEAD_DIM:]], axis=1)], axis=0)

    pairs = [slice(p * PAIR, (p + 1) * PAIR) for p in range(HEADS // 2)]
    scores = [{name: lax.dot_general(qv[name][rows, ps], bdiag(kv[name][rows, ps]), NT,
                                     preferred_element_type=F32) for name in LEVELS}
              for ps in pairs]
    states = [(st_ref[2 * p], st_ref[2 * p + 1]) for p in range(len(pairs))]
    inter = [jnp.dot(qv["c"][rows, ps], sn_ref[p], preferred_element_type=F32)
             for p, ps in enumerate(pairs)]
    upds = [lax.dot_general(jnp.concatenate([v[p][:, :HEAD_DIM], v[p][:, HEAD_DIM:]], axis=0),
                            bdiag(kv["c"][rows, ps]), TN, preferred_element_type=F32)
            for p, ps in enumerate(pairs)]
    for p, ps in enumerate(pairs):
        pm = jnp.zeros((CHUNK, 2 * CHUNK), F32)
        for name in LEVELS:
            pm = jnp.where(masks[name], scores[p][name], pm)
        o_s[rows, ps] = inter[p] + jnp.dot(pm.astype(BF16), bdiag(v[p]),
                                           preferred_element_type=F32)
        g_p = jnp.tile(gt[:, ps], (HEAD_DIM // SUB, 1))
        st0, st1 = states[p]
        st0 = st0 * g_p[:, :HEAD_DIM] + upds[p][:, :HEAD_DIM]
        st1 = st1 * g_p[:, HEAD_DIM:] + upds[p][:, HEAD_DIM:]
        st_ref[2 * p], st_ref[2 * p + 1] = st0, st1
        sn_ref[p, :HEAD_DIM, :HEAD_DIM] = st0.T.astype(BF16)
        sn_ref[p, HEAD_DIM:, HEAD_DIM:] = st1.T.astype(BF16)
    return carry


def _conv_block(r, carry, *, ush_ref, cw_ref, cb_ref, y_s, rows_per):
    r0 = r * rows_per
    nsub = rows_per // SUB
    lead = TAIL - CONV_K + 1
    for lo in range(0, D_MODEL, CONV_LANES):
        lanes = slice(lo, lo + CONV_LANES)
        acc = [cb_ref[:, lanes]] * nsub
        for shift in range(SUB):
            taps = [j for j in range(CONV_K) if (lead + j) % SUB == shift]
            w = {j: cw_ref[j, :, lanes] for j in taps}
            first = min((lead + j) // SUB for j in taps)
            last = max((lead + j) // SUB for j in taps) + nsub - 1
            for m in range(first, last + 1):
                blk = ush_ref[shift, _aligned(r0 + (m + 1) * SUB, SUB), lanes]
                for j in taps:
                    b = m - (lead + j) // SUB
                    if 0 <= b < nsub:
                        acc[b] = acc[b] + blk * w[j]
        for b in range(nsub):
            y_s[_aligned(r0 + b * SUB, SUB), lanes] = acc[b]
    return carry


def _mixer_body(x_ref, xp_ref, mixn_ref, win_ref, lbl_ref, hn_ref, wo_ref, cw_ref, cb_ref, lng_ref,
                lnb_ref, wpw_ref, bpw_ref, wout_ref, o_ref,
                st_ref, sn_ref, tail_ref, ush_ref, gt_s, h_s, hp_s, v_s, q_s, k_s, lf_s, o_s, y_s,
                *var_refs, layer, tiles_per_seq):
    tm = x_ref.shape[0]
    names = LEVELS + ("c",)
    qv = dict(zip(names, var_refs[:len(names)]))
    kv = dict(zip(names, var_refs[len(names):]))

    step = pl.program_id(0)

    @pl.when(step % tiles_per_seq == 0)
    def _():
        st_ref[...] = jnp.zeros_like(st_ref)
        sn_ref[...] = jnp.zeros_like(sn_ref)
        tail_ref[...] = jnp.zeros_like(tail_ref)

    @pl.when(step == 0)
    def _():
        h_s[...] = jnp.zeros_like(h_s)
        o_s[...] = jnp.zeros_like(o_s)
        y_s[...] = jnp.zeros_like(y_s)

    hp_s[...] = h_s[...]

    def proj(i, src=h_s):
        return jnp.dot(src[...], win_ref[:, i * D_MODEL:(i + 1) * D_MODEL],
                       preferred_element_type=F32)

    g_out = proj(3, hp_s)
    gate_a = _sigmoid(proj(6, hp_s))
    gate_b = _sigmoid(proj(7, hp_s))

    x = x_ref[...]
    h_s[...] = _rms(x, mixn_ref[...]).astype(BF16)

    lbl = lbl_ref[...]
    e = jnp.exp(lbl - jnp.max(lbl, axis=0, keepdims=True))
    lb = jnp.sum(e[0:layer + 1], axis=0, keepdims=True) / jnp.sum(e, axis=0, keepdims=True)

    u = proj(4) * _sigmoid(proj(5))
    ext = jnp.concatenate([tail_ref[...], u], axis=0)
    n_ext = TAIL + tm
    tiles = [ext[i:i + SUB] for i in range(0, n_ext, SUB)]
    rid = lax.broadcasted_iota(jnp.int32, (SUB, D_MODEL), 0)
    ush_ref[0, SUB:SUB + n_ext, :] = ext
    rot = tiles
    for r in range(SUB - 1, 0, -1):
        rot = [pltpu.roll(t, 1, 0) for t in rot]
        ush_ref[r, SUB:SUB + n_ext, :] = jnp.concatenate(
            [jnp.where(rid < SUB - r, rot[i], rot[(i + 1) % len(rot)]) for i in range(len(rot))],
            axis=0)
    tail_ref[...] = u[tm - TAIL:tm]

    o = o_s[...]
    hn = hn_ref[...]
    o = jnp.concatenate(
        [_rms(o[:, i * HEAD_DIM:(i + 1) * HEAD_DIM], hn) for i in range(HEADS)], axis=1)
    y_a = jnp.dot((o * (g_out * _sigmoid(g_out))).astype(BF16), wo_ref[...],
                  preferred_element_type=F32)

    f = lb + (1.0 - lb) * _sigmoid(proj(1))
    f = jnp.clip(f, 1e-6, 1.0)
    lf_s[...] = jnp.log(f)
    k_s[...] = 1.0 - f
    q_s[...] = proj(0)
    v_s[...] = proj(2).astype(BF16)

    u = y_s[...]
    mu = jnp.mean(u, axis=-1, keepdims=True)
    uc = u - mu
    var = jnp.mean(uc * uc, axis=-1, keepdims=True)
    u = uc * lax.rsqrt(var + EPS) * lng_ref[...] + lnb_ref[...]
    u = u * _sigmoid(u)
    y_b = jnp.dot(u.astype(BF16), wpw_ref[...], preferred_element_type=F32) + bpw_ref[...]
    merged = gate_a * y_a + gate_b * y_b

    o_ref[...] = xp_ref[...] + jnp.dot(merged.astype(BF16), wout_ref[...],
                                       preferred_element_type=F32)

    n_chunks = tm // CHUNK
    for c in range(n_chunks):
        if c == 0:
            _hgrn_prep(0, 0, q_s=q_s, k_s=k_s, lf_s=lf_s, qv=qv, kv=kv, gt_s=gt_s)
        _hgrn_mm(c, 0, qv=qv, kv=kv, v_s=v_s, o_s=o_s, st_ref=st_ref, sn_ref=sn_ref, gt_s=gt_s)
        if c + 1 < n_chunks:
            _hgrn_prep(c + 1, 0, q_s=q_s, k_s=k_s, lf_s=lf_s, qv=qv, kv=kv, gt_s=gt_s)

    lax.fori_loop(0, tm // CONV_ROWS,
                  functools.partial(_conv_block, ush_ref=ush_ref, cw_ref=cw_ref, cb_ref=cb_ref,
                                    y_s=y_s, rows_per=CONV_ROWS), 0)


def _mixer(x, mixn, win, lbl, hn, wo, cw, cb, lng, lnb, wpw, bpw, wout, *, layer):
    b, l, d = x.shape
    x2d = x.reshape(b * l, d)
    tm = MIX_TM
    assert l % tm == 0
    act = lambda: pltpu.VMEM((tm, d), F32)
    act16 = lambda: pltpu.VMEM((tm, d), BF16)
    n_var = 2 * (len(LEVELS) + 1)
    n_tiles = b * l // tm
    return pl.pallas_call(
        functools.partial(_mixer_body, layer=layer, tiles_per_seq=l // tm),
        grid=(n_tiles + 1,),
        in_specs=[
            pl.BlockSpec((tm, d), lambda s: (jnp.minimum(s, n_tiles - 1), 0)),
            pl.BlockSpec((tm, d), lambda s: (jnp.maximum(s - 1, 0), 0)),
            _resident(mixn.shape), _resident(win.shape), _resident(lbl.shape),
            _resident(hn.shape), _resident(wo.shape), _resident(cw.shape), _resident(cb.shape),
            _resident(lng.shape), _resident(lnb.shape), _resident(wpw.shape),
            _resident(bpw.shape), _resident(wout.shape),
        ],
        out_specs=pl.BlockSpec((tm, d), lambda s: (jnp.maximum(s - 1, 0), 0)),
        out_shape=jax.ShapeDtypeStruct((b * l, d), F32),
        scratch_shapes=[
            pltpu.VMEM((HEADS, HEAD_DIM, HEAD_DIM), F32),
            pltpu.VMEM((HEADS // 2, PAIR, PAIR), BF16),
            pltpu.VMEM((TAIL, d), F32),
            pltpu.VMEM((SUB, SUB + TAIL + tm, d), F32),
            pltpu.VMEM((tm // CHUNK * SUB, d), F32),
            act16(), act16(),
            act16(),
            act(), act(), act(), act(), act(),
        ] + [act16() for _ in range(n_var)],
        compiler_params=pltpu.CompilerParams(
            dimension_semantics=("arbitrary",), vmem_limit_bytes=VMEM_LIMIT),
        name="mixer",
    )(x2d, x2d, mixn, win, lbl, hn, wo, cw, cb, lng, lnb, wpw, bpw, wout).reshape(b, l, d)


def kernel(x, ffn1_norm, ffn1_w_gate, ffn1_w_up, ffn1_w_down, mix_norm, w_in, hgrn_lb_logits, hgrn_head_norm, hgrn_w_o, conv_w, conv_b, conv_ln_g, conv_ln_b, conv_w_pw, conv_b_pw, w_out, ffn2_norm, ffn2_w_gate, ffn2_w_up, ffn2_w_down, final_norm):
    b, l, d = x.shape
    depth = ffn1_norm.shape[0]
    row = lambda a: a.reshape(1, -1)
    sub8 = lambda a: jnp.broadcast_to(a[..., None, :], a.shape[:-1] + (SUB, a.shape[-1]))
    fin = row(final_norm)
    for i in range(depth):
        x, win16, wo16, wpw16, wout16 = _ffn(
            x.reshape(b * l, d), row(ffn1_norm[i]), ffn1_w_gate, ffn1_w_up, ffn1_w_down, fin,
            layer=i, final=False, cast=(w_in, hgrn_w_o, conv_w_pw, w_out))
        x = _mixer(x.reshape(b, l, d), row(mix_norm[i]), win16, hgrn_lb_logits,
                   row(hgrn_head_norm[i]), wo16, sub8(conv_w[i]), sub8(conv_b[i]),
                   row(conv_ln_g[i]), row(conv_ln_b[i]), wpw16, row(conv_b_pw[i]), wout16, layer=i)
        last = i == depth - 1
        x, = _ffn(x.reshape(b * l, d), row(ffn2_norm[i]), ffn2_w_gate, ffn2_w_up, ffn2_w_down, fin,
                  layer=i, final=last)
        x = x.reshape(b, l, d)
    return x
```

```python
import functools

import jax
import jax.numpy as jnp
from jax import lax
from jax.experimental import pallas as pl
from jax.experimental.pallas import tpu as pltpu

D_MODEL = 1024
HEADS = 8
HEAD_DIM = 128
PAIR = 2 * HEAD_DIM
CONV_K = 31
FFN_RES = 0.5
EPS = 1e-6

CHUNK = 64
SUB = 8
NBLK = CHUNK // SUB
TAIL = 32
MID = 3
CONV_ROWS = 128
CONV_LANES = 256
LEVELS = ("d", "8", "16", "32")

FFN_TM = 1024
FFN_ROWS = 256
FFN_COLS = 1536
MIX_TM = 256
VMEM_LIMIT = 60 * 1024 * 1024
VMEM_LIMIT_CAST = 62 * 1024 * 1024

F32 = jnp.float32
BF16 = jnp.bfloat16
NT = (((1,), (1,)), ((), ()))
TN = (((0,), (0,)), ((), ()))


def _rms(x, g):
    ms = jnp.mean(x * x, axis=-1, keepdims=True)
    return x * lax.rsqrt(ms + EPS) * g


def _sigmoid(x):
    return 0.5 * jnp.tanh(0.5 * x) + 0.5


def _aligned(start, n):
    return pl.ds(start if isinstance(start, int) else pl.multiple_of(start, n), n)


def _block_rows(i, n):
    return _aligned(i * n, n)


def _resident(shape):
    nd = len(shape)
    return pl.BlockSpec(shape, lambda *_: (0,) * nd, pipeline_mode=pl.Buffered(1))


def _ffn_body(x_ref, nrm_ref, wg_ref, wu_ref, wd_ref, fin_ref, *rest, final):
    n_cast = (len(rest) - 1) // 2
    o_ref = rest[n_cast]
    for src, dst in zip(rest[:n_cast], rest[n_cast + 1:]):
        dst[...] = src[...].astype(BF16)

    f = wg_ref.shape[1]
    for r0 in range(0, x_ref.shape[0], FFN_ROWS):
        rows = slice(r0, r0 + FFN_ROWS)
        x = x_ref[rows, :]
        h = _rms(x, nrm_ref[...])
        y = x
        for lo in range(0, f, FFN_COLS):
            cols = slice(lo, min(lo + FFN_COLS, f))
            g = jnp.dot(h, wg_ref[:, cols], preferred_element_type=F32)
            u = jnp.dot(h, wu_ref[:, cols], preferred_element_type=F32)
            y = y + FFN_RES * jnp.dot(g * _sigmoid(g) * u, wd_ref[cols, :],
                                      preferred_element_type=F32)
        if final:
            y = _rms(y, fin_ref[...])
        o_ref[rows, :] = y


def _ffn(x2d, nrm, wg, wu, wd, fin, *, layer, final, cast=()):
    t, d = x2d.shape
    f = wg.shape[2]
    tm = FFN_TM
    steps = t // tm
    slab = lambda a: a.shape[1] // steps
    assert all(a.shape[1] % (steps * 16) == 0 for a in cast)
    weight = lambda r, c: pl.BlockSpec((None, r, c), lambda i: (layer, 0, 0),
                                       pipeline_mode=pl.Buffered(1))
    return pl.pallas_call(
        functools.partial(_ffn_body, final=final),
        grid=(t // tm,),
        in_specs=[
            pl.BlockSpec((tm, d), lambda i: (i, 0)),
            _resident((1, d)),
            weight(d, f),
            weight(d, f),
            weight(f, d),
            _resident((1, d)),
        ] + [pl.BlockSpec((None, slab(a), a.shape[2]), lambda i: (layer, i, 0)) for a in cast],
        out_specs=[pl.BlockSpec((tm, d), lambda i: (i, 0))]
        + [pl.BlockSpec((slab(a), a.shape[2]), lambda i: (i, 0)) for a in cast],
        out_shape=[jax.ShapeDtypeStruct((t, d), F32)]
        + [jax.ShapeDtypeStruct(a.shape[1:], BF16) for a in cast],
        compiler_params=pltpu.CompilerParams(
            dimension_semantics=("arbitrary",),
            vmem_limit_bytes=VMEM_LIMIT_CAST if cast else VMEM_LIMIT),
        name="ffn_final" if final else "ffn",
    )(x2d, nrm, wg, wu, wd, fin, *cast)


def _hgrn_prep(c, carry, *, q_s, k_s, lf_s, qv, kv, gt_s):
    rows = _block_rows(c, CHUNK)
    lf, q, k = lf_s[rows, :], q_s[rows, :], k_s[rows, :]
    rid = lax.broadcasted_iota(jnp.int32, (SUB, D_MODEL), 0)

    qe, kf, qd, kd, g = [], [], [], [], []
    for i in range(NBLK):
        sl = slice(i * SUB, (i + 1) * SUB)
        a = lf[sl]
        for s in (1, 2, 4):
            a = a + jnp.where(rid >= s, pltpu.roll(a, s, 0), 0.0)
        tot = jnp.broadcast_to(a[SUB - 1:SUB], (SUB, D_MODEL))
        dm = a - jnp.broadcast_to(a[MID:MID + 1], (SUB, D_MODEL))
        qe.append(q[sl] * jnp.exp(a))
        kf.append(k[sl] * jnp.exp(tot - a))
        qd.append(q[sl] * jnp.exp(dm))
        kd.append(k[sl] * jnp.exp(-dm))
        g.append(jnp.exp(tot))

    def running(blocks):
        out, acc = [None], None
        for blk in blocks:
            acc = blk if acc is None else acc * blk
            out.append(acc)
        return out

    def scaled(base, factors):
        return [b if f is None else b * f for b, f in zip(base, factors)]

    def store(ref, blocks):
        ref[rows, :] = jnp.concatenate(blocks, axis=0).astype(BF16)

    def q_factors(nb):
        out = []
        for lo in range(0, NBLK, nb):
            out += running(g[lo:lo + nb])[:nb]
        return out

    def k_factors(nb):
        out = []
        for lo in range(0, NBLK, nb):
            out += running(g[lo:lo + nb][::-1])[:nb][::-1]
        return out

    store(qv["d"], qd)
    store(kv["d"], kd)
    store(qv["8"], qe)
    store(kv["8"], kf)
    for name, nb in (("16", 2), ("32", 4), ("c", NBLK)):
        store(qv[name], scaled(qe, q_factors(nb)))
        store(kv[name], scaled(kf, k_factors(nb)))
    gt_s[_block_rows(c, SUB), :] = running(g)[NBLK]
    return carry


def _hgrn_mm(c, carry, *, qv, kv, v_s, o_s, st_ref, sn_ref, gt_s):
    rows = _block_rows(c, CHUNK)
    gt = gt_s[_block_rows(c, SUB), :]
    v = [v_s[rows, p * PAIR:(p + 1) * PAIR] for p in range(HEADS // 2)]

    ti = lax.broadcasted_iota(jnp.int32, (CHUNK, 2 * CHUNK), 0)
    si = lax.broadcasted_iota(jnp.int32, (CHUNK, 2 * CHUNK), 1) % CHUNK
    bt, bs = ti // SUB, si // SUB
    masks = {
        "d": (bt == bs) & (si <= ti),
        "8": (bt == bs + 1) & (bt % 2 == 1),
        "16": (bt // 4 == bs // 4) & ((bt // 2) % 2 == 1) & ((bs // 2) % 2 == 0),
        "32": (bt // 4 == 1) & (bs // 4 == 0),
    }

    def bdiag(x):
        z = jnp.zeros((x.shape[0], HEAD_DIM), x.dtype)
        return jnp.concatenate(
            [jnp.concatenate([x[:, :HEAD_DIM], z], axis=1),
             jnp.concatenate([z, x[:, HEAD_DIM:]], axis=1)], axis=0)

    pairs = [slice(p * PAIR, (p + 1) * PAIR) for p in range(HEADS // 2)]
    scores = [{name: lax.dot_general(qv[name][rows, ps], bdiag(kv[name][rows, ps]), NT,
                                     preferred_element_type=F32) for name in LEVELS}
              for ps in pairs]
    states = [(st_ref[2 * p], st_ref[2 * p + 1]) for p in range(len(pairs))]
    inter = [jnp.dot(qv["c"][rows, ps], sn_ref[p], preferred_element_type=F32)
             for p, ps in enumerate(pairs)]
    upds = [lax.dot_general(jnp.concatenate([v[p][:, :HEAD_DIM], v[p][:, HEAD_DIM:]], axis=0),
                            bdiag(kv["c"][rows, ps]), TN, preferred_element_type=F32)
            for p, ps in enumerate(pairs)]
    for p, ps in enumerate(pairs):
        pm = jnp.zeros((CHUNK, 2 * CHUNK), F32)
        for name in LEVELS:
            pm = jnp.where(masks[name], scores[p][name], pm)
        o_s[rows, ps] = inter[p] + jnp.dot(pm.astype(BF16), bdiag(v[p]),
                                           preferred_element_type=F32)
        g_p = jnp.tile(gt[:, ps], (HEAD_DIM // SUB, 1))
        st0, st1 = states[p]
        st0 = st0 * g_p[:, :HEAD_DIM] + upds[p][:, :HEAD_DIM]
        st1 = st1 * g_p[:, HEAD_DIM:] + upds[p][:, HEAD_DIM:]
        st_ref[2 * p], st_ref[2 * p + 1] = st0, st1
        sn_ref[p, :HEAD_DIM, :HEAD_DIM] = st0.T.astype(BF16)
        sn_ref[p, HEAD_DIM:, HEAD_DIM:] = st1.T.astype(BF16)
    return carry


def _conv_block(r, carry, *, ush_ref, cw_ref, cb_ref, y_s, rows_per):
    r0 = r * rows_per
    nsub = rows_per // SUB
    lead = TAIL - CONV_K + 1
    for lo in range(0, D_MODEL, CONV_LANES):
        lanes = slice(lo, lo + CONV_LANES)
        acc = [cb_ref[:, lanes]] * nsub
        for shift in range(SUB):
            taps = [j for j in range(CONV_K) if (lead + j) % SUB == shift]
            w = {j: cw_ref[j, :, lanes] for j in taps}
            first = min((lead + j) // SUB for j in taps)
            last = max((lead + j) // SUB for j in taps) + nsub - 1
            for m in range(first, last + 1):
                blk = ush_ref[shift, _aligned(r0 + (m + 1) * SUB, SUB), lanes]
                for j in taps:
                    b = m - (lead + j) // SUB
                    if 0 <= b < nsub:
                        acc[b] = acc[b] + blk * w[j]
        for b in range(nsub):
            y_s[_aligned(r0 + b * SUB, SUB), lanes] = acc[b]
    return carry


def _mixer_body(x_ref, xp_ref, mixn_ref, win_ref, lbl_ref, hn_ref, wo_ref, cw_ref, cb_ref, lng_ref,
                lnb_ref, wpw_ref, bpw_ref, wout_ref, o_ref,
                st_ref, sn_ref, tail_ref, ush_ref, gt_s, h_s, hp_s, v_s, q_s, k_s, lf_s, o_s, y_s,
                *var_refs, layer, tiles_per_seq):
    tm = x_ref.shape[0]
    names = LEVELS + ("c",)
    qv = dict(zip(names, var_refs[:len(names)]))
    kv = dict(zip(names, var_refs[len(names):]))

    step = pl.program_id(0)

    @pl.when(step % tiles_per_seq == 0)
    def _():
        st_ref[...] = jnp.zeros_like(st_ref)
        sn_ref[...] = jnp.zeros_like(sn_ref)
        tail_ref[...] = jnp.zeros_like(tail_ref)

    @pl.when(step == 0)
    def _():
        h_s[...] = jnp.zeros_like(h_s)
        o_s[...] = jnp.zeros_like(o_s)
        y_s[...] = jnp.zeros_like(y_s)

    hp_s[...] = h_s[...]
    x = x_ref[...]
    h_s[...] = _rms(x, mixn_ref[...]).astype(BF16)

    def proj(i, src=h_s):
        return jnp.dot(src[...], win_ref[:, i * D_MODEL:(i + 1) * D_MODEL],
                       preferred_element_type=F32)

    lbl = lbl_ref[...]
    e = jnp.exp(lbl - jnp.max(lbl, axis=0, keepdims=True))
    lb = jnp.sum(e[0:layer + 1], axis=0, keepdims=True) / jnp.sum(e, axis=0, keepdims=True)

    u = proj(4) * _sigmoid(proj(5))
    ext = jnp.concatenate([tail_ref[...], u], axis=0)
    n_ext = TAIL + tm
    tiles = [ext[i:i + SUB] for i in range(0, n_ext, SUB)]
    rid = lax.broadcasted_iota(jnp.int32, (SUB, D_MODEL), 0)
    ush_ref[0, SUB:SUB + n_ext, :] = ext
    rot = tiles
    for r in range(SUB - 1, 0, -1):
        rot = [pltpu.roll(t, 1, 0) for t in rot]
        ush_ref[r, SUB:SUB + n_ext, :] = jnp.concatenate(
            [jnp.where(rid < SUB - r, rot[i], rot[(i + 1) % len(rot)]) for i in range(len(rot))],
            axis=0)
    tail_ref[...] = u[tm - TAIL:tm]

    o = o_s[...]
    hn = hn_ref[...]
    o = jnp.concatenate(
        [_rms(o[:, i * HEAD_DIM:(i + 1) * HEAD_DIM], hn) for i in range(HEADS)], axis=1)
    g_out = proj(3, hp_s)
    y_a = jnp.dot((o * (g_out * _sigmoid(g_out))).astype(BF16), wo_ref[...],
                  preferred_element_type=F32)

    f = lb + (1.0 - lb) * _sigmoid(proj(1))
    f = jnp.clip(f, 1e-6, 1.0)
    lf_s[...] = jnp.log(f)
    k_s[...] = 1.0 - f
    q_s[...] = proj(0)
    v_s[...] = proj(2).astype(BF16)

    u = y_s[...]
    mu = jnp.mean(u, axis=-1, keepdims=True)
    uc = u - mu
    var = jnp.mean(uc * uc, axis=-1, keepdims=True)
    u = uc * lax.rsqrt(var + EPS) * lng_ref[...] + lnb_ref[...]
    u = u * _sigmoid(u)
    y_b = jnp.dot(u.astype(BF16), wpw_ref[...], preferred_element_type=F32) + bpw_ref[...]
    merged = _sigmoid(proj(6, hp_s)) * y_a + _sigmoid(proj(7, hp_s)) * y_b

    o_ref[...] = xp_ref[...] + jnp.dot(merged.astype(BF16), wout_ref[...],
                                       preferred_element_type=F32)

    n_chunks = tm // CHUNK
    for c in range(n_chunks):
        if c == 0:
            _hgrn_prep(0, 0, q_s=q_s, k_s=k_s, lf_s=lf_s, qv=qv, kv=kv, gt_s=gt_s)
        _hgrn_mm(c, 0, qv=qv, kv=kv, v_s=v_s, o_s=o_s, st_ref=st_ref, sn_ref=sn_ref, gt_s=gt_s)
        if c + 1 < n_chunks:
            _hgrn_prep(c + 1, 0, q_s=q_s, k_s=k_s, lf_s=lf_s, qv=qv, kv=kv, gt_s=gt_s)

    lax.fori_loop(0, tm // CONV_ROWS,
                  functools.partial(_conv_block, ush_ref=ush_ref, cw_ref=cw_ref, cb_ref=cb_ref,
                                    y_s=y_s, rows_per=CONV_ROWS), 0)


def _mixer(x, mixn, win, lbl, hn, wo, cw, cb, lng, lnb, wpw, bpw, wout, *, layer):
    b, l, d = x.shape
    x2d = x.reshape(b * l, d)
    tm = MIX_TM
    assert l % tm == 0
    act = lambda: pltpu.VMEM((tm, d), F32)
    act16 = lambda: pltpu.VMEM((tm, d), BF16)
    n_var = 2 * (len(LEVELS) + 1)
    n_tiles = b * l // tm
    return pl.pallas_call(
        functools.partial(_mixer_body, layer=layer, tiles_per_seq=l // tm),
        grid=(n_tiles + 1,),
        in_specs=[
            pl.BlockSpec((tm, d), lambda s: (jnp.minimum(s, n_tiles - 1), 0)),
            pl.BlockSpec((tm, d), lambda s: (jnp.maximum(s - 1, 0), 0)),
            _resident(mixn.shape), _resident(win.shape), _resident(lbl.shape),
            _resident(hn.shape), _resident(wo.shape), _resident(cw.shape), _resident(cb.shape),
            _resident(lng.shape), _resident(lnb.shape), _resident(wpw.shape),
            _resident(bpw.shape), _resident(wout.shape),
        ],
        out_specs=pl.BlockSpec((tm, d), lambda s: (jnp.maximum(s - 1, 0), 0)),
        out_shape=jax.ShapeDtypeStruct((b * l, d), F32),
        scratch_shapes=[
            pltpu.VMEM((HEADS, HEAD_DIM, HEAD_DIM), F32),
            pltpu.VMEM((HEADS // 2, PAIR, PAIR), BF16),
            pltpu.VMEM((TAIL, d), F32),
            pltpu.VMEM((SUB, SUB + TAIL + tm, d), F32),
            pltpu.VMEM((tm // CHUNK * SUB, d), F32),
            act16(), act16(),
            act16(),
            act(), act(), act(), act(), act(),
        ] + [act16() for _ in range(n_var)],
        compiler_params=pltpu.CompilerParams(
            dimension_semantics=("arbitrary",), vmem_limit_bytes=VMEM_LIMIT),
        name="mixer",
    )(x2d, x2d, mixn, win, lbl, hn, wo, cw, cb, lng, lnb, wpw, bpw, wout).reshape(b, l, d)


def kernel(x, ffn1_norm, ffn1_w_gate, ffn1_w_up, ffn1_w_down, mix_norm, w_in, hgrn_lb_logits, hgrn_head_norm, hgrn_w_o, conv_w, conv_b, conv_ln_g, conv_ln_b, conv_w_pw, conv_b_pw, w_out, ffn2_norm, ffn2_w_gate, ffn2_w_up, ffn2_w_down, final_norm):
    b, l, d = x.shape
    depth = ffn1_norm.shape[0]
    row = lambda a: a.reshape(1, -1)
    sub8 = lambda a: jnp.broadcast_to(a[..., None, :], a.shape[:-1] + (SUB, a.shape[-1]))
    fin = row(final_norm)
    for i in range(depth):
        x, win16, wo16, wpw16, wout16 = _ffn(
            x.reshape(b * l, d), row(ffn1_norm[i]), ffn1_w_gate, ffn1_w_up, ffn1_w_down, fin,
            layer=i, final=False, cast=(w_in, hgrn_w_o, conv_w_pw, w_out))
        x = _mixer(x.reshape(b, l, d), row(mix_norm[i]), win16, hgrn_lb_logits,
                   row(hgrn_head_norm[i]), wo16, sub8(conv_w[i]), sub8(conv_b[i]),
                   row(conv_ln_g[i]), row(conv_ln_b[i]), wpw16, row(conv_b_pw[i]), wout16, layer=i)
        last = i == depth - 1
        x, = _ffn(x.reshape(b * l, d), row(ffn2_norm[i]), ffn2_w_gate, ffn2_w_up, ffn2_w_down, fin,
                  layer=i, final=last)
        x = x.reshape(b, l, d)
    return x
```

```python
import functools

import jax
import jax.numpy as jnp
from jax import lax
from jax.experimental import pallas as pl
from jax.experimental.pallas import tpu as pltpu

D_MODEL = 1024
D_FF = 2816
HEADS = 8
HEAD_DIM = 128
PAIR = 2 * HEAD_DIM
CONV_K = 31
FFN_RES = 0.5
EPS = 1e-6
N_SPLITS = 8

CHUNK = 64
SUB = 8
NBLK = CHUNK // SUB
TAIL = 32
MID = 3
CONV_ROWS = 128
CONV_LANES = 256
LEVELS = ("d", "8", "16", "32")

FFN_TM = 1024
FFN_ROWS = 256
FFN_COLS = 1536
MIX_TM = 256
VMEM_LIMIT = 60 * 1024 * 1024
VMEM_LIMIT_CAST = 62 * 1024 * 1024

F32 = jnp.float32
BF16 = jnp.bfloat16
NT = (((1,), (1,)), ((), ()))
TN = (((0,), (0,)), ((), ()))


def _rms(x, g):
    ms = jnp.mean(x * x, axis=-1, keepdims=True)
    return x * lax.rsqrt(ms + EPS) * g


def _sigmoid(x):
    return 0.5 * jnp.tanh(0.5 * x) + 0.5


def _aligned(start, n):
    return pl.ds(start if isinstance(start, int) else pl.multiple_of(start, n), n)


def _block_rows(i, n):
    return _aligned(i * n, n)


def _resident(shape):
    nd = len(shape)
    return pl.BlockSpec(shape, lambda *_: (0,) * nd, pipeline_mode=pl.Buffered(1))


def _ffn_body(x_ref, nrm_ref, wg_ref, wu_ref, wd_ref, fin_ref, *rest, final):
    n_cast = (len(rest) - 1) // 2
    o_ref = rest[n_cast]
    for src, dst in zip(rest[:n_cast], rest[n_cast + 1:]):
        dst[...] = src[...].astype(BF16)

    f = wg_ref.shape[1]
    starts = list(range(0, x_ref.shape[0], FFN_ROWS))
    norm = lambda r0: _rms(x_ref[r0:r0 + FFN_ROWS, :], nrm_ref[...])
    h_next = norm(starts[0])
    for g_idx, r0 in enumerate(starts):
        rows = slice(r0, r0 + FFN_ROWS)
        x = x_ref[rows, :]
        h = h_next
        if g_idx + 1 < len(starts):
            h_next = norm(starts[g_idx + 1])
        y = x
        for lo in range(0, f, FFN_COLS):
            cols = slice(lo, min(lo + FFN_COLS, f))
            g = jnp.dot(h, wg_ref[:, cols], preferred_element_type=F32)
            u = jnp.dot(h, wu_ref[:, cols], preferred_element_type=F32)
            y = y + FFN_RES * jnp.dot(g * _sigmoid(g) * u, wd_ref[cols, :],
                                      preferred_element_type=F32)
        if final:
            y = _rms(y, fin_ref[...])
        o_ref[rows, :] = y


def _ffn(x2d, nrm, wg, wu, wd, fin, *, layer, final, cast=()):
    t, d = x2d.shape
    f = wg.shape[2]
    tm = FFN_TM
    steps = t // tm
    slab = lambda a: a.shape[1] // steps
    assert all(a.shape[1] % (steps * 16) == 0 for a in cast)
    weight = lambda r, c: pl.BlockSpec((None, r, c), lambda i: (layer, 0, 0),
                                       pipeline_mode=pl.Buffered(1))
    return pl.pallas_call(
        functools.partial(_ffn_body, final=final),
        grid=(t // tm,),
        in_specs=[
            pl.BlockSpec((tm, d), lambda i: (i, 0)),
            _resident((1, d)),
            weight(d, f),
            weight(d, f),
            weight(f, d),
            _resident((1, d)),
        ] + [pl.BlockSpec((None, slab(a), a.shape[2]), lambda i: (layer, i, 0)) for a in cast],
        out_specs=[pl.BlockSpec((tm, d), lambda i: (i, 0))]
        + [pl.BlockSpec((slab(a), a.shape[2]), lambda i: (i, 0)) for a in cast],
        out_shape=[jax.ShapeDtypeStruct((t, d), F32)]
        + [jax.ShapeDtypeStruct(a.shape[1:], BF16) for a in cast],
        compiler_params=pltpu.CompilerParams(
            dimension_semantics=("arbitrary",),
            vmem_limit_bytes=VMEM_LIMIT_CAST if cast else VMEM_LIMIT),
        name="ffn_final" if final else "ffn",
    )(x2d, nrm, wg, wu, wd, fin, *cast)


def _hgrn_prep(c, carry, *, q_s, k_s, lf_s, qv, kv, gt_s):
    rows = _block_rows(c, CHUNK)
    lf, q, k = lf_s[rows, :], q_s[rows, :], k_s[rows, :]
    rid = lax.broadcasted_iota(jnp.int32, (SUB, D_MODEL), 0)

    qe, kf, qd, kd, g = [], [], [], [], []
    for i in range(NBLK):
        sl = slice(i * SUB, (i + 1) * SUB)
        a = lf[sl]
        for s in (1, 2, 4):
            a = a + jnp.where(rid >= s, pltpu.roll(a, s, 0), 0.0)
        tot = jnp.broadcast_to(a[SUB - 1:SUB], (SUB, D_MODEL))
        dm = a - jnp.broadcast_to(a[MID:MID + 1], (SUB, D_MODEL))
        qe.append(q[sl] * jnp.exp(a))
        kf.append(k[sl] * jnp.exp(tot - a))
        qd.append(q[sl] * jnp.exp(dm))
        kd.append(k[sl] * jnp.exp(-dm))
        g.append(jnp.exp(tot))

    def running(blocks):
        out, acc = [None], None
        for blk in blocks:
            acc = blk if acc is None else acc * blk
            out.append(acc)
        return out

    def scaled(base, factors):
        return [b if f is None else b * f for b, f in zip(base, factors)]

    def store(ref, blocks):
        ref[rows, :] = jnp.concatenate(blocks, axis=0).astype(BF16)

    def q_factors(nb):
        out = []
        for lo in range(0, NBLK, nb):
            out += running(g[lo:lo + nb])[:nb]
        return out

    def k_factors(nb):
        out = []
        for lo in range(0, NBLK, nb):
            out += running(g[lo:lo + nb][::-1])[:nb][::-1]
        return out

    store(qv["d"], qd)
    store(kv["d"], kd)
    store(qv["8"], qe)
    store(kv["8"], kf)
    for name, nb in (("16", 2), ("32", 4), ("c", NBLK)):
        store(qv[name], scaled(qe, q_factors(nb)))
        store(kv[name], scaled(kf, k_factors(nb)))
    gt_s[_block_rows(c, SUB), :] = running(g)[NBLK]
    return carry


def _hgrn_mm(c, carry, *, qv, kv, v_s, o_s, st_ref, sn_ref, gt_s):
    rows = _block_rows(c, CHUNK)
    gt = gt_s[_block_rows(c, SUB), :]
    v = [v_s[rows, p * PAIR:(p + 1) * PAIR] for p in range(HEADS // 2)]

    ti = lax.broadcasted_iota(jnp.int32, (CHUNK, 2 * CHUNK), 0)
    si = lax.broadcasted_iota(jnp.int32, (CHUNK, 2 * CHUNK), 1) % CHUNK
    bt, bs = ti // SUB, si // SUB
    masks = {
        "d": (bt == bs) & (si <= ti),
        "8": (bt == bs + 1) & (bt % 2 == 1),
        "16": (bt // 4 == bs // 4) & ((bt // 2) % 2 == 1) & ((bs // 2) % 2 == 0),
        "32": (bt // 4 == 1) & (bs // 4 == 0),
    }

    def bdiag(x):
        z = jnp.zeros((x.shape[0], HEAD_DIM), x.dtype)
        return jnp.concatenate(
            [jnp.concatenate([x[:, :HEAD_DIM], z], axis=1),
             jnp.concatenate([z, x[:, HEAD_DIM:]], axis=1)], axis=0)

    pairs = [slice(p * PAIR, (p + 1) * PAIR) for p in range(HEADS // 2)]
    scores = [{name: lax.dot_general(qv[name][rows, ps], bdiag(kv[name][rows, ps]), NT,
                                     preferred_element_type=F32) for name in LEVELS}
              for ps in pairs]
    states = [(st_ref[2 * p], st_ref[2 * p + 1]) for p in range(len(pairs))]
    inter = [jnp.dot(qv["c"][rows, ps], sn_ref[p], preferred_element_type=F32)
             for p, ps in enumerate(pairs)]
    upds = [lax.dot_general(jnp.concatenate([v[p][:, :HEAD_DIM], v[p][:, HEAD_DIM:]], axis=0),
                            bdiag(kv["c"][rows, ps]), TN, preferred_element_type=F32)
            for p, ps in enumerate(pairs)]
    for p, ps in enumerate(pairs):
        pm = jnp.zeros((CHUNK, 2 * CHUNK), F32)
        for name in LEVELS:
            pm = jnp.where(masks[name], scores[p][name], pm)
        o_s[rows, ps] = inter[p] + jnp.dot(pm.astype(BF16), bdiag(v[p]),
                                           preferred_element_type=F32)
        g_p = jnp.tile(gt[:, ps], (HEAD_DIM // SUB, 1))
        st0, st1 = states[p]
        st0 = st0 * g_p[:, :HEAD_DIM] + upds[p][:, :HEAD_DIM]
        st1 = st1 * g_p[:, HEAD_DIM:] + upds[p][:, HEAD_DIM:]
        st_ref[2 * p], st_ref[2 * p + 1] = st0, st1
        sn_ref[p, :HEAD_DIM, :HEAD_DIM] = st0.T.astype(BF16)
        sn_ref[p, HEAD_DIM:, HEAD_DIM:] = st1.T.astype(BF16)
    return carry


def _conv_block(r, carry, *, ush_ref, cw_ref, cb_ref, y_s, rows_per):
    r0 = r * rows_per
    nsub = rows_per // SUB
    lead = TAIL - CONV_K + 1
    for lo in range(0, D_MODEL, CONV_LANES):
        lanes = slice(lo, lo + CONV_LANES)
        acc = [cb_ref[:, lanes]] * nsub
        for shift in range(SUB):
            taps = [j for j in range(CONV_K) if (lead + j) % SUB == shift]
            w = {j: cw_ref[j, :, lanes] for j in taps}
            first = min((lead + j) // SUB for j in taps)
            last = max((lead + j) // SUB for j in taps) + nsub - 1
            for m in range(first, last + 1):
                blk = ush_ref[shift, _aligned(r0 + (m + 1) * SUB, SUB), lanes]
                for j in taps:
                    b = m - (lead + j) // SUB
                    if 0 <= b < nsub:
                        acc[b] = acc[b] + blk * w[j]
        for b in range(nsub):
            y_s[_aligned(r0 + b * SUB, SUB), lanes] = acc[b]
    return carry


def _mixer_body(x_ref, xp_ref, mixn_ref, win_ref, lbl_ref, hn_ref, wo_ref, cw_ref, cb_ref, lng_ref,
                lnb_ref, wpw_ref, bpw_ref, wout_ref, o_ref,
                st_ref, sn_ref, tail_ref, ush_ref, gt_s, h_s, hp_s, v_s, q_s, k_s, lf_s, o_s, y_s,
                *var_refs, layer, tiles_per_seq):
    tm = x_ref.shape[0]
    names = LEVELS + ("c",)
    qv = dict(zip(names, var_refs[:len(names)]))
    kv = dict(zip(names, var_refs[len(names):]))

    step = pl.program_id(0)

    @pl.when(step % tiles_per_seq == 0)
    def _():
        st_ref[...] = jnp.zeros_like(st_ref)
        sn_ref[...] = jnp.zeros_like(sn_ref)
        tail_ref[...] = jnp.zeros_like(tail_ref)

    @pl.when(step == 0)
    def _():
        h_s[...] = jnp.zeros_like(h_s)
        o_s[...] = jnp.zeros_like(o_s)
        y_s[...] = jnp.zeros_like(y_s)

    hp_s[...] = h_s[...]
    x = x_ref[...]
    h_s[...] = _rms(x, mixn_ref[...]).astype(BF16)

    def proj(i, src=h_s):
        return jnp.dot(src[...], win_ref[:, i * D_MODEL:(i + 1) * D_MODEL],
                       preferred_element_type=F32)

    lbl = lbl_ref[...]
    e = jnp.exp(lbl - jnp.max(lbl, axis=0, keepdims=True))
    lb = jnp.sum(e[0:layer + 1], axis=0, keepdims=True) / jnp.sum(e, axis=0, keepdims=True)

    u = proj(4) * _sigmoid(proj(5))
    ext = jnp.concatenate([tail_ref[...], u], axis=0)
    n_ext = TAIL + tm
    tiles = [ext[i:i + SUB] for i in range(0, n_ext, SUB)]
    rid = lax.broadcasted_iota(jnp.int32, (SUB, D_MODEL), 0)
    ush_ref[0, SUB:SUB + n_ext, :] = ext
    rot = tiles
    for r in range(SUB - 1, 0, -1):
        rot = [pltpu.roll(t, 1, 0) for t in rot]
        ush_ref[r, SUB:SUB + n_ext, :] = jnp.concatenate(
            [jnp.where(rid < SUB - r, rot[i], rot[(i + 1) % len(rot)]) for i in range(len(rot))],
            axis=0)
    tail_ref[...] = u[tm - TAIL:tm]

    o = o_s[...]
    hn = hn_ref[...]
    o = jnp.concatenate(
        [_rms(o[:, i * HEAD_DIM:(i + 1) * HEAD_DIM], hn) for i in range(HEADS)], axis=1)
    g_out = proj(3, hp_s)
    y_a = jnp.dot((o * (g_out * _sigmoid(g_out))).astype(BF16), wo_ref[...],
                  preferred_element_type=F32)

    f = lb + (1.0 - lb) * _sigmoid(proj(1))
    f = jnp.clip(f, 1e-6, 1.0)
    lf_s[...] = jnp.log(f)
    k_s[...] = 1.0 - f
    q_s[...] = proj(0)
    v_s[...] = proj(2).astype(BF16)

    u = y_s[...]
    mu = jnp.mean(u, axis=-1, keepdims=True)
    uc = u - mu
    var = jnp.mean(uc * uc, axis=-1, keepdims=True)
    u = uc * lax.rsqrt(var + EPS) * lng_ref[...] + lnb_ref[...]
    u = u * _sigmoid(u)
    y_b = jnp.dot(u.astype(BF16), wpw_ref[...], preferred_element_type=F32) + bpw_ref[...]
    merged = _sigmoid(proj(6, hp_s)) * y_a + _sigmoid(proj(7, hp_s)) * y_b

    o_ref[...] = xp_ref[...] + jnp.dot(merged.astype(BF16), wout_ref[...],
                                       preferred_element_type=F32)

    n_chunks = tm // CHUNK
    for c in range(n_chunks):
        if c == 0:
            _hgrn_prep(0, 0, q_s=q_s, k_s=k_s, lf_s=lf_s, qv=qv, kv=kv, gt_s=gt_s)
        _hgrn_mm(c, 0, qv=qv, kv=kv, v_s=v_s, o_s=o_s, st_ref=st_ref, sn_ref=sn_ref, gt_s=gt_s)
        if c + 1 < n_chunks:
            _hgrn_prep(c + 1, 0, q_s=q_s, k_s=k_s, lf_s=lf_s, qv=qv, kv=kv, gt_s=gt_s)

    lax.fori_loop(0, tm // CONV_ROWS,
                  functools.partial(_conv_block, ush_ref=ush_ref, cw_ref=cw_ref, cb_ref=cb_ref,
                                    y_s=y_s, rows_per=CONV_ROWS), 0)


def _mixer(x, mixn, win, lbl, hn, wo, cw, cb, lng, lnb, wpw, bpw, wout, *, layer):
    b, l, d = x.shape
    x2d = x.reshape(b * l, d)
    tm = MIX_TM
    assert l % tm == 0
    act = lambda: pltpu.VMEM((tm, d), F32)
    act16 = lambda: pltpu.VMEM((tm, d), BF16)
    n_var = 2 * (len(LEVELS) + 1)
    n_tiles = b * l // tm
    return pl.pallas_call(
        functools.partial(_mixer_body, layer=layer, tiles_per_seq=l // tm),
        grid=(n_tiles + 1,),
        in_specs=[
            pl.BlockSpec((tm, d), lambda s: (jnp.minimum(s, n_tiles - 1), 0)),
            pl.BlockSpec((tm, d), lambda s: (jnp.maximum(s - 1, 0), 0)),
            _resident(mixn.shape), _resident(win.shape), _resident(lbl.shape),
            _resident(hn.shape), _resident(wo.shape), _resident(cw.shape), _resident(cb.shape),
            _resident(lng.shape), _resident(lnb.shape), _resident(wpw.shape),
            _resident(bpw.shape), _resident(wout.shape),
        ],
        out_specs=pl.BlockSpec((tm, d), lambda s: (jnp.maximum(s - 1, 0), 0)),
        out_shape=jax.ShapeDtypeStruct((b * l, d), F32),
        scratch_shapes=[
            pltpu.VMEM((HEADS, HEAD_DIM, HEAD_DIM), F32),
            pltpu.VMEM((HEADS // 2, PAIR, PAIR), BF16),
            pltpu.VMEM((TAIL, d), F32),
            pltpu.VMEM((SUB, SUB + TAIL + tm, d), F32),
            pltpu.VMEM((tm // CHUNK * SUB, d), F32),
            act16(), act16(),
            act16(),
            act(), act(), act(), act(), act(),
        ] + [act16() for _ in range(n_var)],
        compiler_params=pltpu.CompilerParams(
            dimension_semantics=("arbitrary",), vmem_limit_bytes=VMEM_LIMIT),
        name="mixer",
    )(x2d, x2d, mixn, win, lbl, hn, wo, cw, cb, lng, lnb, wpw, bpw, wout).reshape(b, l, d)


def kernel(x, ffn1_norm, ffn1_w_gate, ffn1_w_up, ffn1_w_down, mix_norm, w_in, hgrn_lb_logits, hgrn_head_norm, hgrn_w_o, conv_w, conv_b, conv_ln_g, conv_ln_b, conv_w_pw, conv_b_pw, w_out, ffn2_norm, ffn2_w_gate, ffn2_w_up, ffn2_w_down, final_norm):
    b, l, d = x.shape
    depth = ffn1_norm.shape[0]
    row = lambda a: a.reshape(1, -1)
    sub8 = lambda a: jnp.broadcast_to(a[..., None, :], a.shape[:-1] + (SUB, a.shape[-1]))
    fin = row(final_norm)
    for i in range(depth):
        x, win16, wo16, wpw16, wout16 = _ffn(
            x.reshape(b * l, d), row(ffn1_norm[i]), ffn1_w_gate, ffn1_w_up, ffn1_w_down, fin,
            layer=i, final=False, cast=(w_in, hgrn_w_o, conv_w_pw, w_out))
        x = _mixer(x.reshape(b, l, d), row(mix_norm[i]), win16, hgrn_lb_logits,
                   row(hgrn_head_norm[i]), wo16, sub8(conv_w[i]), sub8(conv_b[i]),
                   row(conv_ln_g[i]), row(conv_ln_b[i]), wpw16, row(conv_b_pw[i]), wout16, layer=i)
        last = i == depth - 1
        x, = _ffn(x.reshape(b * l, d), row(ffn2_norm[i]), ffn2_w_gate, ffn2_w_up, ffn2_w_down, fin,
                  layer=i, final=last)
        x = x.reshape(b, l, d)
    return x
```

```python
import functools

import jax
import jax.numpy as jnp
from jax import lax
from jax.experimental import pallas as pl
from jax.experimental.pallas import tpu as pltpu

D_MODEL = 1024
D_FF = 2816
HEADS = 8
HEAD_DIM = 128
PAIR = 2 * HEAD_DIM
CONV_K = 31
FFN_RES = 0.5
EPS = 1e-6
N_SPLITS = 8

CHUNK = 64
SUB = 8
NBLK = CHUNK // SUB
TAIL = 32
MID = 3
CONV_ROWS = 128
CONV_LANES = 256
LEVELS = ("d", "8", "16", "32")

FFN_TM = 1024
FFN_ROWS = 256
FFN_COLS = 1024
MIX_TM = 256
VMEM_LIMIT = 60 * 1024 * 1024
VMEM_LIMIT_CAST = 62 * 1024 * 1024

F32 = jnp.float32
BF16 = jnp.bfloat16
NT = (((1,), (1,)), ((), ()))
TN = (((0,), (0,)), ((), ()))


def _rms(x, g):
    ms = jnp.mean(x * x, axis=-1, keepdims=True)
    return x * lax.rsqrt(ms + EPS) * g


def _sigmoid(x):
    return 0.5 * jnp.tanh(0.5 * x) + 0.5


def _aligned(start, n):
    return pl.ds(start if isinstance(start, int) else pl.multiple_of(start, n), n)


def _block_rows(i, n):
    return _aligned(i * n, n)


def _resident(shape):
    nd = len(shape)
    return pl.BlockSpec(shape, lambda *_: (0,) * nd, pipeline_mode=pl.Buffered(1))


def _ffn_body(x_ref, nrm_ref, wg_ref, wu_ref, wd_ref, fin_ref, *rest, final):
    n_cast = (len(rest) - 1) // 2
    o_ref = rest[n_cast]
    for src, dst in zip(rest[:n_cast], rest[n_cast + 1:]):
        dst[...] = src[...].astype(BF16)

    f = wg_ref.shape[1]
    for r0 in range(0, x_ref.shape[0], FFN_ROWS):
        rows = slice(r0, r0 + FFN_ROWS)
        x = x_ref[rows, :]
        h = _rms(x, nrm_ref[...])
        y = x
        for lo in range(0, f, FFN_COLS):
            cols = slice(lo, min(lo + FFN_COLS, f))
            g = jnp.dot(h, wg_ref[:, cols], preferred_element_type=F32)
            u = jnp.dot(h, wu_ref[:, cols], preferred_element_type=F32)
            y = y + FFN_RES * jnp.dot(g * _sigmoid(g) * u, wd_ref[cols, :],
                                      preferred_element_type=F32)
        if final:
            y = _rms(y, fin_ref[...])
        o_ref[rows, :] = y


def _ffn(x2d, nrm, wg, wu, wd, fin, *, layer, final, cast=()):
    t, d = x2d.shape
    f = wg.shape[2]
    tm = FFN_TM
    steps = t // tm
    slab = lambda a: a.shape[1] // steps
    assert all(a.shape[1] % (steps * 16) == 0 for a in cast)
    weight = lambda r, c: pl.BlockSpec((None, r, c), lambda i: (layer, 0, 0),
                                       pipeline_mode=pl.Buffered(1))
    return pl.pallas_call(
        functools.partial(_ffn_body, final=final),
        grid=(t // tm,),
        in_specs=[
            pl.BlockSpec((tm, d), lambda i: (i, 0)),
            _resident((1, d)),
            weight(d, f),
            weight(d, f),
            weight(f, d),
            _resident((1, d)),
        ] + [pl.BlockSpec((None, slab(a), a.shape[2]), lambda i: (layer, i, 0)) for a in cast],
        out_specs=[pl.BlockSpec((tm, d), lambda i: (i, 0))]
        + [pl.BlockSpec((slab(a), a.shape[2]), lambda i: (i, 0)) for a in cast],
        out_shape=[jax.ShapeDtypeStruct((t, d), F32)]
        + [jax.ShapeDtypeStruct(a.shape[1:], BF16) for a in cast],
        compiler_params=pltpu.CompilerParams(
            dimension_semantics=("arbitrary",),
            vmem_limit_bytes=VMEM_LIMIT_CAST if cast else VMEM_LIMIT),
        name="ffn_final" if final else "ffn",
    )(x2d, nrm, wg, wu, wd, fin, *cast)


def _hgrn_prep(c, carry, *, q_s, k_s, lf_s, qv, kv, gt_s):
    rows = _block_rows(c, CHUNK)
    lf, q, k = lf_s[rows, :], q_s[rows, :], k_s[rows, :]
    rid = lax.broadcasted_iota(jnp.int32, (SUB, D_MODEL), 0)

    qe, kf, qd, kd, g = [], [], [], [], []
    for i in range(NBLK):
        sl = slice(i * SUB, (i + 1) * SUB)
        a = lf[sl]
        for s in (1, 2, 4):
            a = a + jnp.where(rid >= s, pltpu.roll(a, s, 0), 0.0)
        tot = jnp.broadcast_to(a[SUB - 1:SUB], (SUB, D_MODEL))
        dm = a - jnp.broadcast_to(a[MID:MID + 1], (SUB, D_MODEL))
        qe.append(q[sl] * jnp.exp(a))
        kf.append(k[sl] * jnp.exp(tot - a))
        qd.append(q[sl] * jnp.exp(dm))
        kd.append(k[sl] * jnp.exp(-dm))
        g.append(jnp.exp(tot))

    def running(blocks):
        out, acc = [None], None
        for blk in blocks:
            acc = blk if acc is None else acc * blk
            out.append(acc)
        return out

    def scaled(base, factors):
        return [b if f is None else b * f for b, f in zip(base, factors)]

    def store(ref, blocks):
        ref[rows, :] = jnp.concatenate(blocks, axis=0).astype(BF16)

    def q_factors(nb):
        out = []
        for lo in range(0, NBLK, nb):
            out += running(g[lo:lo + nb])[:nb]
        return out

    def k_factors(nb):
        out = []
        for lo in range(0, NBLK, nb):
            out += running(g[lo:lo + nb][::-1])[:nb][::-1]
        return out

    store(qv["d"], qd)
    store(kv["d"], kd)
    store(qv["8"], qe)
    store(kv["8"], kf)
    for name, nb in (("16", 2), ("32", 4), ("c", NBLK)):
        store(qv[name], scaled(qe, q_factors(nb)))
        store(kv[name], scaled(kf, k_factors(nb)))
    gt_s[_block_rows(c, SUB), :] = running(g)[NBLK]
    return carry


def _hgrn_mm(c, carry, *, qv, kv, v_s, o_s, st_ref, sn_ref, gt_s):
    rows = _block_rows(c, CHUNK)
    gt = gt_s[_block_rows(c, SUB), :]
    v = [v_s[rows, p * PAIR:(p + 1) * PAIR] for p in range(HEADS // 2)]

    ti = lax.broadcasted_iota(jnp.int32, (CHUNK, 2 * CHUNK), 0)
    si = lax.broadcasted_iota(jnp.int32, (CHUNK, 2 * CHUNK), 1) % CHUNK
    bt, bs = ti // SUB, si // SUB
    masks = {
        "d": (bt == bs) & (si <= ti),
        "8": (bt == bs + 1) & (bt % 2 == 1),
        "16": (bt // 4 == bs // 4) & ((bt // 2) % 2 == 1) & ((bs // 2) % 2 == 0),
        "32": (bt // 4 == 1) & (bs // 4 == 0),
    }

    def bdiag(x):
        z = jnp.zeros((x.shape[0], HEAD_DIM), x.dtype)
        return jnp.concatenate(
            [jnp.concatenate([x[:, :HEAD_DIM], z], axis=1),
             jnp.concatenate([z, x[:, HEAD_DIM:]], axis=1)], axis=0)

    pairs = [slice(p * PAIR, (p + 1) * PAIR) for p in range(HEADS // 2)]
    scores = [{name: lax.dot_general(qv[name][rows, ps], bdiag(kv[name][rows, ps]), NT,
                                     preferred_element_type=F32) for name in LEVELS}
              for ps in pairs]
    states = [(st_ref[2 * p], st_ref[2 * p + 1]) for p in range(len(pairs))]
    inter = [jnp.dot(qv["c"][rows, ps], sn_ref[p], preferred_element_type=F32)
             for p, ps in enumerate(pairs)]
    upds = [lax.dot_general(jnp.concatenate([v[p][:, :HEAD_DIM], v[p][:, HEAD_DIM:]], axis=0),
                            bdiag(kv["c"][rows, ps]), TN, preferred_element_type=F32)
            for p, ps in enumerate(pairs)]
    for p, ps in enumerate(pairs):
        pm = jnp.zeros((CHUNK, 2 * CHUNK), F32)
        for name in LEVELS:
            pm = jnp.where(masks[name], scores[p][name], pm)
        o_s[rows, ps] = inter[p] + jnp.dot(pm.astype(BF16), bdiag(v[p]),
                                           preferred_element_type=F32)
        g_p = jnp.tile(gt[:, ps], (HEAD_DIM // SUB, 1))
        st0, st1 = states[p]
        st0 = st0 * g_p[:, :HEAD_DIM] + upds[p][:, :HEAD_DIM]
        st1 = st1 * g_p[:, HEAD_DIM:] + upds[p][:, HEAD_DIM:]
        st_ref[2 * p], st_ref[2 * p + 1] = st0, st1
        sn_ref[p, :HEAD_DIM, :HEAD_DIM] = st0.T.astype(BF16)
        sn_ref[p, HEAD_DIM:, HEAD_DIM:] = st1.T.astype(BF16)
    return carry


def _conv_block(r, carry, *, ush_ref, cw_ref, cb_ref, y_s, rows_per):
    r0 = r * rows_per
    nsub = rows_per // SUB
    lead = TAIL - CONV_K + 1
    for lo in range(0, D_MODEL, CONV_LANES):
        lanes = slice(lo, lo + CONV_LANES)
        acc = [cb_ref[:, lanes]] * nsub
        for shift in range(SUB):
            taps = [j for j in range(CONV_K) if (lead + j) % SUB == shift]
            w = {j: cw_ref[j, :, lanes] for j in taps}
            first = min((lead + j) // SUB for j in taps)
            last = max((lead + j) // SUB for j in taps) + nsub - 1
            for m in range(first, last + 1):
                blk = ush_ref[shift, _aligned(r0 + (m + 1) * SUB, SUB), lanes]
                for j in taps:
                    b = m - (lead + j) // SUB
                    if 0 <= b < nsub:
                        acc[b] = acc[b] + blk * w[j]
        for b in range(nsub):
            y_s[_aligned(r0 + b * SUB, SUB), lanes] = acc[b]
    return carry


def _mixer_body(x_ref, xp_ref, mixn_ref, win_ref, lbl_ref, hn_ref, wo_ref, cw_ref, cb_ref, lng_ref,
                lnb_ref, wpw_ref, bpw_ref, wout_ref, o_ref,
                st_ref, sn_ref, tail_ref, ush_ref, gt_s, h_s, hp_s, v_s, q_s, k_s, lf_s, o_s, y_s,
                *var_refs, layer, tiles_per_seq):
    tm = x_ref.shape[0]
    names = LEVELS + ("c",)
    qv = dict(zip(names, var_refs[:len(names)]))
    kv = dict(zip(names, var_refs[len(names):]))

    step = pl.program_id(0)

    @pl.when(step % tiles_per_seq == 0)
    def _():
        st_ref[...] = jnp.zeros_like(st_ref)
        sn_ref[...] = jnp.zeros_like(sn_ref)
        tail_ref[...] = jnp.zeros_like(tail_ref)

    @pl.when(step == 0)
    def _():
        h_s[...] = jnp.zeros_like(h_s)
        o_s[...] = jnp.zeros_like(o_s)
        y_s[...] = jnp.zeros_like(y_s)

    hp_s[...] = h_s[...]
    x = x_ref[...]
    h_s[...] = _rms(x, mixn_ref[...]).astype(BF16)

    def proj(i, src=h_s):
        return jnp.dot(src[...], win_ref[:, i * D_MODEL:(i + 1) * D_MODEL],
                       preferred_element_type=F32)

    lbl = lbl_ref[...]
    e = jnp.exp(lbl - jnp.max(lbl, axis=0, keepdims=True))
    lb = jnp.sum(e[0:layer + 1], axis=0, keepdims=True) / jnp.sum(e, axis=0, keepdims=True)

    u = proj(4) * _sigmoid(proj(5))
    ext = jnp.concatenate([tail_ref[...], u], axis=0)
    n_ext = TAIL + tm
    tiles = [ext[i:i + SUB] for i in range(0, n_ext, SUB)]
    rid = lax.broadcasted_iota(jnp.int32, (SUB, D_MODEL), 0)
    ush_ref[0, SUB:SUB + n_ext, :] = ext
    rot = tiles
    for r in range(SUB - 1, 0, -1):
        rot = [pltpu.roll(t, 1, 0) for t in rot]
        ush_ref[r, SUB:SUB + n_ext, :] = jnp.concatenate(
            [jnp.where(rid < SUB - r, rot[i], rot[(i + 1) % len(rot)]) for i in range(len(rot))],
            axis=0)
    tail_ref[...] = u[tm - TAIL:tm]

    o = o_s[...]
    hn = hn_ref[...]
    o = jnp.concatenate(
        [_rms(o[:, i * HEAD_DIM:(i + 1) * HEAD_DIM], hn) for i in range(HEADS)], axis=1)
    g_out = proj(3, hp_s)
    y_a = jnp.dot((o * (g_out * _sigmoid(g_out))).astype(BF16), wo_ref[...],
                  preferred_element_type=F32)

    f = lb + (1.0 - lb) * _sigmoid(proj(1))
    f = jnp.clip(f, 1e-6, 1.0)
    lf_s[...] = jnp.log(f)
    k_s[...] = 1.0 - f
    q_s[...] = proj(0)
    v_s[...] = proj(2).astype(BF16)

    u = y_s[...]
    mu = jnp.mean(u, axis=-1, keepdims=True)
    uc = u - mu
    var = jnp.mean(uc * uc, axis=-1, keepdims=True)
    u = uc * lax.rsqrt(var + EPS) * lng_ref[...] + lnb_ref[...]
    u = u * _sigmoid(u)
    y_b = jnp.dot(u.astype(BF16), wpw_ref[...], preferred_element_type=F32) + bpw_ref[...]
    merged = _sigmoid(proj(6, hp_s)) * y_a + _sigmoid(proj(7, hp_s)) * y_b

    o_ref[...] = xp_ref[...] + jnp.dot(merged.astype(BF16), wout_ref[...],
                                       preferred_element_type=F32)

    n_chunks = tm // CHUNK
    for c in range(n_chunks):
        if c == 0:
            _hgrn_prep(0, 0, q_s=q_s, k_s=k_s, lf_s=lf_s, qv=qv, kv=kv, gt_s=gt_s)
        _hgrn_mm(c, 0, qv=qv, kv=kv, v_s=v_s, o_s=o_s, st_ref=st_ref, sn_ref=sn_ref, gt_s=gt_s)
        if c + 1 < n_chunks:
            _hgrn_prep(c + 1, 0, q_s=q_s, k_s=k_s, lf_s=lf_s, qv=qv, kv=kv, gt_s=gt_s)

    lax.fori_loop(0, tm // CONV_ROWS,
                  functools.partial(_conv_block, ush_ref=ush_ref, cw_ref=cw_ref, cb_ref=cb_ref,
                                    y_s=y_s, rows_per=CONV_ROWS), 0)


def _mixer(x, mixn, win, lbl, hn, wo, cw, cb, lng, lnb, wpw, bpw, wout, *, layer):
    b, l, d = x.shape
    x2d = x.reshape(b * l, d)
    tm = MIX_TM
    assert l % tm == 0
    act = lambda: pltpu.VMEM((tm, d), F32)
    act16 = lambda: pltpu.VMEM((tm, d), BF16)
    n_var = 2 * (len(LEVELS) + 1)
    n_tiles = b * l // tm
    return pl.pallas_call(
        functools.partial(_mixer_body, layer=layer, tiles_per_seq=l // tm),
        grid=(n_tiles + 1,),
        in_specs=[
            pl.BlockSpec((tm, d), lambda s: (jnp.minimum(s, n_tiles - 1), 0)),
            pl.BlockSpec((tm, d), lambda s: (jnp.maximum(s - 1, 0), 0)),
            _resident(mixn.shape), _resident(win.shape), _resident(lbl.shape),
            _resident(hn.shape), _resident(wo.shape), _resident(cw.shape), _resident(cb.shape),
            _resident(lng.shape), _resident(lnb.shape), _resident(wpw.shape),
            _resident(bpw.shape), _resident(wout.shape),
        ],
        out_specs=pl.BlockSpec((tm, d), lambda s: (jnp.maximum(s - 1, 0), 0)),
        out_shape=jax.ShapeDtypeStruct((b * l, d), F32),
        scratch_shapes=[
            pltpu.VMEM((HEADS, HEAD_DIM, HEAD_DIM), F32),
            pltpu.VMEM((HEADS // 2, PAIR, PAIR), BF16),
            pltpu.VMEM((TAIL, d), F32),
            pltpu.VMEM((SUB, SUB + TAIL + tm, d), F32),
            pltpu.VMEM((tm // CHUNK * SUB, d), F32),
            act16(), act16(),
            act16(),
            act(), act(), act(), act(), act(),
        ] + [act16() for _ in range(n_var)],
        compiler_params=pltpu.CompilerParams(
            dimension_semantics=("arbitrary",), vmem_limit_bytes=VMEM_LIMIT),
        name="mixer",
    )(x2d, x2d, mixn, win, lbl, hn, wo, cw, cb, lng, lnb, wpw, bpw, wout).reshape(b, l, d)


def kernel(x, ffn1_norm, ffn1_w_gate, ffn1_w_up, ffn1_w_down, mix_norm, w_in, hgrn_lb_logits, hgrn_head_norm, hgrn_w_o, conv_w, conv_b, conv_ln_g, conv_ln_b, conv_w_pw, conv_b_pw, w_out, ffn2_norm, ffn2_w_gate, ffn2_w_up, ffn2_w_down, final_norm):
    b, l, d = x.shape
    depth = ffn1_norm.shape[0]
    row = lambda a: a.reshape(1, -1)
    sub8 = lambda a: jnp.broadcast_to(a[..., None, :], a.shape[:-1] + (SUB, a.shape[-1]))
    fin = row(final_norm)
    for i in range(depth):
        x, win16, wo16, wpw16, wout16 = _ffn(
            x.reshape(b * l, d), row(ffn1_norm[i]), ffn1_w_gate, ffn1_w_up, ffn1_w_down, fin,
            layer=i, final=False, cast=(w_in, hgrn_w_o, conv_w_pw, w_out))
        x = _mixer(x.reshape(b, l, d), row(mix_norm[i]), win16, hgrn_lb_logits,
                   row(hgrn_head_norm[i]), wo16, sub8(conv_w[i]), sub8(conv_b[i]),
                   row(conv_ln_g[i]), row(conv_ln_b[i]), wpw16, row(conv_b_pw[i]), wout16, layer=i)
        last = i == depth - 1
        x, = _ffn(x.reshape(b * l, d), row(ffn2_norm[i]), ffn2_w_gate, ffn2_w_up, ffn2_w_down, fin,
                  layer=i, final=last)
        x = x.reshape(b, l, d)
    return x
```

```python
import functools

import jax
import jax.numpy as jnp
from jax import lax
from jax.experimental import pallas as pl
from jax.experimental.pallas import tpu as pltpu

D_MODEL = 1024
D_FF = 2816
HEADS = 8
HEAD_DIM = 128
PAIR = 2 * HEAD_DIM
CONV_K = 31
FFN_RES = 0.5
EPS = 1e-6
N_SPLITS = 8

CHUNK = 64
SUB = 8
NBLK = CHUNK // SUB
TAIL = 32
MID = 3
CONV_ROWS = 128
CONV_LANES = 256
LEVELS = ("d", "8", "16", "32")

FFN_TM = 1024
FFN_ROWS = 256
FFN_COLS = 1536
MIX_TM = 256
VMEM_LIMIT = 60 * 1024 * 1024
VMEM_LIMIT_CAST = 62 * 1024 * 1024

F32 = jnp.float32
BF16 = jnp.bfloat16
NT = (((1,), (1,)), ((), ()))
TN = (((0,), (0,)), ((), ()))


def _rms(x, g):
    ms = jnp.mean(x * x, axis=-1, keepdims=True)
    return x * lax.rsqrt(ms + EPS) * g


def _sigmoid(x):
    return 0.5 * jnp.tanh(0.5 * x) + 0.5


def _aligned(start, n):
    return pl.ds(start if isinstance(start, int) else pl.multiple_of(start, n), n)


def _block_rows(i, n):
    return _aligned(i * n, n)


def _resident(shape):
    nd = len(shape)
    return pl.BlockSpec(shape, lambda *_: (0,) * nd, pipeline_mode=pl.Buffered(1))


def _ffn_body(x_ref, nrm_ref, wg_ref, wu_ref, wd_ref, fin_ref, *rest, final):
    n_cast = (len(rest) - 1) // 2
    o_ref = rest[n_cast]
    for src, dst in zip(rest[:n_cast], rest[n_cast + 1:]):
        dst[...] = src[...].astype(BF16)

    f = wg_ref.shape[1]
    for r0 in range(0, x_ref.shape[0], FFN_ROWS):
        rows = slice(r0, r0 + FFN_ROWS)
        x = x_ref[rows, :]
        h = _rms(x, nrm_ref[...])
        down = None
        for lo in range(0, f, FFN_COLS):
            cols = slice(lo, min(lo + FFN_COLS, f))
            g = jnp.dot(h, wg_ref[:, cols], preferred_element_type=F32)
            u = jnp.dot(h, wu_ref[:, cols], preferred_element_type=F32)
            part = jnp.dot(g * _sigmoid(g) * u, wd_ref[cols, :], preferred_element_type=F32)
            down = part if down is None else down + part
        y = x + FFN_RES * down
        if final:
            y = _rms(y, fin_ref[...])
        o_ref[rows, :] = y


def _ffn(x2d, nrm, wg, wu, wd, fin, *, layer, final, cast=()):
    t, d = x2d.shape
    f = wg.shape[2]
    tm = FFN_TM
    steps = t // tm
    slab = lambda a: a.shape[1] // steps
    assert all(a.shape[1] % (steps * 16) == 0 for a in cast)
    weight = lambda r, c: pl.BlockSpec((None, r, c), lambda i: (layer, 0, 0),
                                       pipeline_mode=pl.Buffered(1))
    return pl.pallas_call(
        functools.partial(_ffn_body, final=final),
        grid=(t // tm,),
        in_specs=[
            pl.BlockSpec((tm, d), lambda i: (i, 0)),
            _resident((1, d)),
            weight(d, f),
            weight(d, f),
            weight(f, d),
            _resident((1, d)),
        ] + [pl.BlockSpec((None, slab(a), a.shape[2]), lambda i: (layer, i, 0)) for a in cast],
        out_specs=[pl.BlockSpec((tm, d), lambda i: (i, 0))]
        + [pl.BlockSpec((slab(a), a.shape[2]), lambda i: (i, 0)) for a in cast],
        out_shape=[jax.ShapeDtypeStruct((t, d), F32)]
        + [jax.ShapeDtypeStruct(a.shape[1:], BF16) for a in cast],
        compiler_params=pltpu.CompilerParams(
            dimension_semantics=("arbitrary",),
            vmem_limit_bytes=VMEM_LIMIT_CAST if cast else VMEM_LIMIT),
        name="ffn_final" if final else "ffn",
    )(x2d, nrm, wg, wu, wd, fin, *cast)


def _hgrn_prep(c, carry, *, q_s, k_s, lf_s, qv, kv, gt_s):
    rows = _block_rows(c, CHUNK)
    lf, q, k = lf_s[rows, :], q_s[rows, :], k_s[rows, :]
    rid = lax.broadcasted_iota(jnp.int32, (SUB, D_MODEL), 0)

    qe, kf, qd, kd, g = [], [], [], [], []
    for i in range(NBLK):
        sl = slice(i * SUB, (i + 1) * SUB)
        a = lf[sl]
        for s in (1, 2, 4):
            a = a + jnp.where(rid >= s, pltpu.roll(a, s, 0), 0.0)
        tot = jnp.broadcast_to(a[SUB - 1:SUB], (SUB, D_MODEL))
        dm = a - jnp.broadcast_to(a[MID:MID + 1], (SUB, D_MODEL))
        qe.append(q[sl] * jnp.exp(a))
        kf.append(k[sl] * jnp.exp(tot - a))
        qd.append(q[sl] * jnp.exp(dm))
        kd.append(k[sl] * jnp.exp(-dm))
        g.append(jnp.exp(tot))

    def running(blocks):
        out, acc = [None], None
        for blk in blocks:
            acc = blk if acc is None else acc * blk
            out.append(acc)
        return out

    def scaled(base, factors):
        return [b if f is None else b * f for b, f in zip(base, factors)]

    def store(ref, blocks):
        ref[rows, :] = jnp.concatenate(blocks, axis=0).astype(BF16)

    def q_factors(nb):
        out = []
        for lo in range(0, NBLK, nb):
            out += running(g[lo:lo + nb])[:nb]
        return out

    def k_factors(nb):
        out = []
        for lo in range(0, NBLK, nb):
            out += running(g[lo:lo + nb][::-1])[:nb][::-1]
        return out

    store(qv["d"], qd)
    store(kv["d"], kd)
    store(qv["8"], qe)
    store(kv["8"], kf)
    for name, nb in (("16", 2), ("32", 4), ("c", NBLK)):
        store(qv[name], scaled(qe, q_factors(nb)))
        store(kv[name], scaled(kf, k_factors(nb)))
    gt_s[_block_rows(c, SUB), :] = running(g)[NBLK]
    return carry


def _hgrn_mm(c, carry, *, qv, kv, v_s, o_s, st_ref, sn_ref, gt_s):
    rows = _block_rows(c, CHUNK)
    gt = gt_s[_block_rows(c, SUB), :]
    v = [v_s[rows, p * PAIR:(p + 1) * PAIR] for p in range(HEADS // 2)]

    ti = lax.broadcasted_iota(jnp.int32, (CHUNK, 2 * CHUNK), 0)
    si = lax.broadcasted_iota(jnp.int32, (CHUNK, 2 * CHUNK), 1) % CHUNK
    bt, bs = ti // SUB, si // SUB
    masks = {
        "d": (bt == bs) & (si <= ti),
        "8": (bt == bs + 1) & (bt % 2 == 1),
        "16": (bt // 4 == bs // 4) & ((bt // 2) % 2 == 1) & ((bs // 2) % 2 == 0),
        "32": (bt // 4 == 1) & (bs // 4 == 0),
    }

    def bdiag(x):
        z = jnp.zeros((x.shape[0], HEAD_DIM), x.dtype)
        return jnp.concatenate(
            [jnp.concatenate([x[:, :HEAD_DIM], z], axis=1),
             jnp.concatenate([z, x[:, HEAD_DIM:]], axis=1)], axis=0)

    pairs = [slice(p * PAIR, (p + 1) * PAIR) for p in range(HEADS // 2)]
    scores = [{name: lax.dot_general(qv[name][rows, ps], bdiag(kv[name][rows, ps]), NT,
                                     preferred_element_type=F32) for name in LEVELS}
              for ps in pairs]
    states = [(st_ref[2 * p], st_ref[2 * p + 1]) for p in range(len(pairs))]
    inter = [jnp.dot(qv["c"][rows, ps], sn_ref[p], preferred_element_type=F32)
             for p, ps in enumerate(pairs)]
    upds = [lax.dot_general(jnp.concatenate([v[p][:, :HEAD_DIM], v[p][:, HEAD_DIM:]], axis=0),
                            bdiag(kv["c"][rows, ps]), TN, preferred_element_type=F32)
            for p, ps in enumerate(pairs)]
    for p, ps in enumerate(pairs):
        pm = jnp.zeros((CHUNK, 2 * CHUNK), F32)
        for name in LEVELS:
            pm = jnp.where(masks[name], scores[p][name], pm)
        o_s[rows, ps] = inter[p] + jnp.dot(pm.astype(BF16), bdiag(v[p]),
                                           preferred_element_type=F32)
        g_p = jnp.tile(gt[:, ps], (HEAD_DIM // SUB, 1))
        st0, st1 = states[p]
        st0 = st0 * g_p[:, :HEAD_DIM] + upds[p][:, :HEAD_DIM]
        st1 = st1 * g_p[:, HEAD_DIM:] + upds[p][:, HEAD_DIM:]
        st_ref[2 * p], st_ref[2 * p + 1] = st0, st1
        sn_ref[p, :HEAD_DIM, :HEAD_DIM] = st0.T.astype(BF16)
        sn_ref[p, HEAD_DIM:, HEAD_DIM:] = st1.T.astype(BF16)
    return carry


def _conv_block(r, carry, *, ush_ref, cw_ref, cb_ref, y_s, rows_per):
    r0 = r * rows_per
    nsub = rows_per // SUB
    lead = TAIL - CONV_K + 1
    for lo in range(0, D_MODEL, CONV_LANES):
        lanes = slice(lo, lo + CONV_LANES)
        acc = [cb_ref[:, lanes]] * nsub
        for shift in range(SUB):
            taps = [j for j in range(CONV_K) if (lead + j) % SUB == shift]
            w = {j: cw_ref[j, :, lanes] for j in taps}
            first = min((lead + j) // SUB for j in taps)
            last = max((lead + j) // SUB for j in taps) + nsub - 1
            for m in range(first, last + 1):
                blk = ush_ref[shift, _aligned(r0 + (m + 1) * SUB, SUB), lanes]
                for j in taps:
                    b = m - (lead + j) // SUB
                    if 0 <= b < nsub:
                        acc[b] = acc[b] + blk * w[j]
        for b in range(nsub):
            y_s[_aligned(r0 + b * SUB, SUB), lanes] = acc[b]
    return carry


def _mixer_body(x_ref, xp_ref, mixn_ref, win_ref, lbl_ref, hn_ref, wo_ref, cw_ref, cb_ref, lng_ref,
                lnb_ref, wpw_ref, bpw_ref, wout_ref, o_ref,
                st_ref, sn_ref, tail_ref, ush_ref, gt_s, h_s, hp_s, v_s, q_s, k_s, lf_s, o_s, y_s,
                *var_refs, layer, tiles_per_seq):
    tm = x_ref.shape[0]
    names = LEVELS + ("c",)
    qv = dict(zip(names, var_refs[:len(names)]))
    kv = dict(zip(names, var_refs[len(names):]))

    step = pl.program_id(0)

    @pl.when(step % tiles_per_seq == 0)
    def _():
        st_ref[...] = jnp.zeros_like(st_ref)
        sn_ref[...] = jnp.zeros_like(sn_ref)
        tail_ref[...] = jnp.zeros_like(tail_ref)

    @pl.when(step == 0)
    def _():
        h_s[...] = jnp.zeros_like(h_s)
        o_s[...] = jnp.zeros_like(o_s)
        y_s[...] = jnp.zeros_like(y_s)

    hp_s[...] = h_s[...]
    x = x_ref[...]
    h_s[...] = _rms(x, mixn_ref[...]).astype(BF16)

    def proj(i, src=h_s):
        return jnp.dot(src[...], win_ref[:, i * D_MODEL:(i + 1) * D_MODEL],
                       preferred_element_type=F32)

    lbl = lbl_ref[...]
    e = jnp.exp(lbl - jnp.max(lbl, axis=0, keepdims=True))
    lb = jnp.sum(e[0:layer + 1], axis=0, keepdims=True) / jnp.sum(e, axis=0, keepdims=True)

    u = proj(4) * _sigmoid(proj(5))
    ext = jnp.concatenate([tail_ref[...], u], axis=0)
    n_ext = TAIL + tm
    tiles = [ext[i:i + SUB] for i in range(0, n_ext, SUB)]
    rid = lax.broadcasted_iota(jnp.int32, (SUB, D_MODEL), 0)
    ush_ref[0, SUB:SUB + n_ext, :] = ext
    rot = tiles
    for r in range(SUB - 1, 0, -1):
        rot = [pltpu.roll(t, 1, 0) for t in rot]
        ush_ref[r, SUB:SUB + n_ext, :] = jnp.concatenate(
            [jnp.where(rid < SUB - r, rot[i], rot[(i + 1) % len(rot)]) for i in range(len(rot))],
            axis=0)
    tail_ref[...] = u[tm - TAIL:tm]

    o = o_s[...]
    hn = hn_ref[...]
    o = jnp.concatenate(
        [_rms(o[:, i * HEAD_DIM:(i + 1) * HEAD_DIM], hn) for i in range(HEADS)], axis=1)
    g_out = proj(3, hp_s)
    y_a = jnp.dot((o * (g_out * _sigmoid(g_out))).astype(BF16), wo_ref[...],
                  preferred_element_type=F32)

    f = lb + (1.0 - lb) * _sigmoid(proj(1))
    f = jnp.clip(f, 1e-6, 1.0)
    lf_s[...] = jnp.log(f)
    k_s[...] = 1.0 - f
    q_s[...] = proj(0)
    v_s[...] = proj(2).astype(BF16)

    u = y_s[...]
    mu = jnp.mean(u, axis=-1, keepdims=True)
    uc = u - mu
    var = jnp.mean(uc * uc, axis=-1, keepdims=True)
    u = uc * lax.rsqrt(var + EPS) * lng_ref[...] + lnb_ref[...]
    u = u * _sigmoid(u)
    y_b = jnp.dot(u.astype(BF16), wpw_ref[...], preferred_element_type=F32) + bpw_ref[...]
    merged = _sigmoid(proj(6, hp_s)) * y_a + _sigmoid(proj(7, hp_s)) * y_b

    o_ref[...] = xp_ref[...] + jnp.dot(merged.astype(BF16), wout_ref[...],
                                       preferred_element_type=F32)

    n_chunks = tm // CHUNK
    for c in range(n_chunks):
        if c == 0:
            _hgrn_prep(0, 0, q_s=q_s, k_s=k_s, lf_s=lf_s, qv=qv, kv=kv, gt_s=gt_s)
        _hgrn_mm(c, 0, qv=qv, kv=kv, v_s=v_s, o_s=o_s, st_ref=st_ref, sn_ref=sn_ref, gt_s=gt_s)
        if c + 1 < n_chunks:
            _hgrn_prep(c + 1, 0, q_s=q_s, k_s=k_s, lf_s=lf_s, qv=qv, kv=kv, gt_s=gt_s)

    lax.fori_loop(0, tm // CONV_ROWS,
                  functools.partial(_conv_block, ush_ref=ush_ref, cw_ref=cw_ref, cb_ref=cb_ref,
                                    y_s=y_s, rows_per=CONV_ROWS), 0)


def _mixer(x, mixn, win, lbl, hn, wo, cw, cb, lng, lnb, wpw, bpw, wout, *, layer):
    b, l, d = x.shape
    x2d = x.reshape(b * l, d)
    tm = MIX_TM
    assert l % tm == 0
    act = lambda: pltpu.VMEM((tm, d), F32)
    act16 = lambda: pltpu.VMEM((tm, d), BF16)
    n_var = 2 * (len(LEVELS) + 1)
    n_tiles = b * l // tm
    return pl.pallas_call(
        functools.partial(_mixer_body, layer=layer, tiles_per_seq=l // tm),
        grid=(n_tiles + 1,),
        in_specs=[
            pl.BlockSpec((tm, d), lambda s: (jnp.minimum(s, n_tiles - 1), 0)),
            pl.BlockSpec((tm, d), lambda s: (jnp.maximum(s - 1, 0), 0)),
            _resident(mixn.shape), _resident(win.shape), _resident(lbl.shape),
            _resident(hn.shape), _resident(wo.shape), _resident(cw.shape), _resident(cb.shape),
            _resident(lng.shape), _resident(lnb.shape), _resident(wpw.shape),
            _resident(bpw.shape), _resident(wout.shape),
        ],
        out_specs=pl.BlockSpec((tm, d), lambda s: (jnp.maximum(s - 1, 0), 0)),
        out_shape=jax.ShapeDtypeStruct((b * l, d), F32),
        scratch_shapes=[
            pltpu.VMEM((HEADS, HEAD_DIM, HEAD_DIM), F32),
            pltpu.VMEM((HEADS // 2, PAIR, PAIR), BF16),
            pltpu.VMEM((TAIL, d), F32),
            pltpu.VMEM((SUB, SUB + TAIL + tm, d), F32),
            pltpu.VMEM((tm // CHUNK * SUB, d), F32),
            act16(), act16(),
            act16(),
            act(), act(), act(), act(), act(),
        ] + [act16() for _ in range(n_var)],
        compiler_params=pltpu.CompilerParams(
            dimension_semantics=("arbitrary",), vmem_limit_bytes=VMEM_LIMIT),
        name="mixer",
    )(x2d, x2d, mixn, win, lbl, hn, wo, cw, cb, lng, lnb, wpw, bpw, wout).reshape(b, l, d)


def kernel(x, ffn1_norm, ffn1_w_gate, ffn1_w_up, ffn1_w_down, mix_norm, w_in, hgrn_lb_logits, hgrn_head_norm, hgrn_w_o, conv_w, conv_b, conv_ln_g, conv_ln_b, conv_w_pw, conv_b_pw, w_out, ffn2_norm, ffn2_w_gate, ffn2_w_up, ffn2_w_down, final_norm):
    b, l, d = x.shape
    depth = ffn1_norm.shape[0]
    row = lambda a: a.reshape(1, -1)
    sub8 = lambda a: jnp.broadcast_to(a[..., None, :], a.shape[:-1] + (SUB, a.shape[-1]))
    fin = row(final_norm)
    for i in range(depth):
        x, win16, wo16, wpw16, wout16 = _ffn(
            x.reshape(b * l, d), row(ffn1_norm[i]), ffn1_w_gate, ffn1_w_up, ffn1_w_down, fin,
            layer=i, final=False, cast=(w_in, hgrn_w_o, conv_w_pw, w_out))
        x = _mixer(x.reshape(b, l, d), row(mix_norm[i]), win16, hgrn_lb_logits,
                   row(hgrn_head_norm[i]), wo16, sub8(conv_w[i]), sub8(conv_b[i]),
                   row(conv_ln_g[i]), row(conv_ln_b[i]), wpw16, row(conv_b_pw[i]), wout16, layer=i)
        last = i == depth - 1
        x, = _ffn(x.reshape(b * l, d), row(ffn2_norm[i]), ffn2_w_gate, ffn2_w_up, ffn2_w_down, fin,
                  layer=i, final=last)
        x = x.reshape(b, l, d)
    return x
```
